```python
import math
import jax, jax.numpy as jnp
from jax import lax
import numpy as np

D_MODEL = 1024
BATCH = 8
SEQ = 4096
DEPTH = 4

HEAD_DIM = 64
GRID_W = 64
ROPE_THETA = 10000.0
EPS = 1e-6
BLOCK = 128
NEG = -1e30

WIN_HEADS = 4
WIN_KV = 2
WIN_RADIUS = 128
AX_HEADS = 4
AX_KV = 2
DIL_HEADS = 4
DIL_KV = 2
DIL_PATTERNS = ((128, 1), (512, 4), (2048, 16))
MLA_HEADS = 4
MLA_Q_RANK = 256
MLA_KV_RANK = 128
MLA_NOPE = 64
MLA_ROPE = 32
MLA_V = 64

N_GROUPS = 4
GROUP_W = 4 * HEAD_DIM
MIX_WIDTH = N_GROUPS * GROUP_W
IN_SIZES = (WIN_HEADS * HEAD_DIM, WIN_KV * HEAD_DIM, WIN_KV * HEAD_DIM,
            AX_HEADS * HEAD_DIM, AX_KV * HEAD_DIM, AX_KV * HEAD_DIM,
            DIL_HEADS * HEAD_DIM, DIL_KV * HEAD_DIM, DIL_KV * HEAD_DIM,
            MLA_Q_RANK, MLA_KV_RANK, MLA_ROPE)
IN_COLS = sum(IN_SIZES)

PEER_KEYS = 128
PEER_N = PEER_KEYS * PEER_KEYS
PEER_HEADS = 8
PEER_DK = 256
PEER_TOPK = 16
PEER_CHUNK = 128

kernel_name = "hybrid_parallel_heads_peer_encoder"


def rms_norm(x, g):
    xf = x.astype(jnp.float32)
    y = xf * lax.rsqrt(jnp.mean(xf * xf, axis=-1, keepdims=True) + EPS)
    return (y * g.astype(jnp.float32)).astype(x.dtype)


def rope_angles(pos, dim):
    inv = ROPE_THETA ** (-jnp.arange(0, dim, 2, dtype=jnp.float32) / dim)
    ang = pos.astype(jnp.float32)[:, None] * inv[None, :]
    return jnp.cos(ang), jnp.sin(ang)


def apply_rope(x, cos, sin):
    half = x.shape[-1] // 2
    xf = x.astype(jnp.float32)
    x1, x2 = xf[..., :half], xf[..., half:]
    c, s = cos[None, :, None, :], sin[None, :, None, :]
    return jnp.concatenate([x1 * c - x2 * s, x2 * c + x1 * s], axis=-1).astype(x.dtype)


def dense_attention_blocks(q, k, v, scale):
    b, s_len, kv, g, dk = q.shape
    nb = s_len // BLOCK
    qb = q.reshape(b, nb, BLOCK, kv, g, dk).transpose(1, 0, 2, 3, 4, 5)

    def one(qblk):
        sc = jnp.einsum('bqkgd,bskd->bkgqs', qblk, k, preferred_element_type=jnp.float32) * scale
        p = jax.nn.softmax(sc, axis=-1).astype(v.dtype)
        return jnp.einsum('bkgqs,bskd->bqkgd', p, v)

    o = lax.map(one, qb)
    return o.transpose(1, 0, 2, 3, 4, 5).reshape(b, s_len, kv, g, v.shape[-1])


def windowed_attention_sink(q, k, v, sink):
    b, s_len, kv, g, hd = q.shape
    nb = s_len // BLOCK
    qb = q.reshape(b, nb, BLOCK, kv, g, hd)

    def neighbours(t):
        tb = t.reshape(b, nb, BLOCK, kv, t.shape[-1])
        pad = jnp.zeros_like(tb[:, :1])
        tp = jnp.concatenate([pad, tb, pad], axis=1)
        return jnp.concatenate([tp[:, :-2], tp[:, 1:-1], tp[:, 2:]], axis=2)

    kw, vw = neighbours(k), neighbours(v)
    qpos = jnp.arange(nb)[:, None] * BLOCK + jnp.arange(BLOCK)[None, :]
    kpos = jnp.arange(nb)[:, None] * BLOCK + jnp.arange(-BLOCK, 2 * BLOCK)[None, :]
    rel = kpos[:, None, :] - qpos[:, :, None]
    valid = (jnp.abs(rel) <= WIN_RADIUS) & (kpos[:, None, :] >= 0) & (kpos[:, None, :] < s_len)
    sc = jnp.einsum('bnqkgd,bnskd->bnkgqs', qb, kw, preferred_element_type=jnp.float32) * (hd ** -0.5)
    sc = jnp.where(valid[None, :, None, None], sc, NEG)
    sink_l = sink.astype(jnp.float32).reshape(kv, g)[None, None, :, :, None, None]
    m = jnp.maximum(jnp.max(sc, axis=-1, keepdims=True), sink_l)
    e = jnp.exp(sc - m)
    p = (e / (jnp.sum(e, axis=-1, keepdims=True) + jnp.exp(sink_l - m))).astype(v.dtype)
    o = jnp.einsum('bnkgqs,bnskd->bnqkgd', p, vw)
    return o.reshape(b, s_len, kv, g, hd)


def dilated_attention(q, k, v):
    b, s_len, kv, g, hd = q.shape
    nb = s_len // BLOCK
    offs_np = np.stack([np.concatenate([-np.arange(w // 2, 0, -r), np.zeros(1, np.int64), np.arange(r, w // 2 + 1, r)])
                        for (w, r) in DIL_PATTERNS]).astype(np.int32)
    offs = jnp.asarray(offs_np)
    qb = q.reshape(b, nb, BLOCK, kv, g, hd).transpose(1, 0, 2, 3, 4, 5)

    def one(args):
        qblk, q0 = args
        qpos = q0 + jnp.arange(BLOCK)
        kpos = qpos[None, :, None] + offs[:, None, :]
        valid = (kpos >= 0) & (kpos < s_len)
        kidx = jnp.clip(kpos, 0, s_len - 1)
        kg = k[:, kidx]
        vg = v[:, kidx]
        sc = jnp.einsum('bqkgd,bpqjkd->bpkgqj', qblk, kg, preferred_element_type=jnp.float32) * (hd ** -0.5)
        sc = jnp.where(valid[None, :, None, None, :, :], sc, NEG)
        lse = jax.nn.logsumexp(sc, axis=-1)
        p = jnp.exp(sc - lse[..., None]).astype(v.dtype)
        o = jnp.einsum('bpkgqj,bpqjkd->bpqkgd', p, vg)
        wts = jax.nn.softmax(lse, axis=1).astype(o.dtype)
        return jnp.einsum('bpkgq,bpqkgd->bqkgd', wts, o)

    o = lax.map(one, (qb, jnp.arange(nb) * BLOCK))
    return o.transpose(1, 0, 2, 3, 4, 5).reshape(b, s_len, kv, g, hd)


def mixer_window(q, k, v, sink, cos, sin):
    b, s_len, _ = q.shape
    q = apply_rope(q.reshape(b, s_len, WIN_HEADS, HEAD_DIM), cos, sin).reshape(b, s_len, WIN_KV, WIN_HEADS // WIN_KV, HEAD_DIM)
    k = apply_rope(k.reshape(b, s_len, WIN_KV, HEAD_DIM), cos, sin)
    v = v.reshape(b, s_len, WIN_KV, HEAD_DIM)
    return windowed_attention_sink(q, k, v, sink).reshape(b, s_len, WIN_HEADS * HEAD_DIM)


def rope_2d(x, cos_r, sin_r, cos_c, sin_c):
    half = x.shape[-1] // 2
    return jnp.concatenate([apply_rope(x[..., :half], cos_r, sin_r), apply_rope(x[..., half:], cos_c, sin_c)], axis=-1)


def mixer_axial(q, k, v, g_qn, g_kn, cos_r, sin_r, cos_c, sin_c):
    b, s_len, _ = q.shape
    q = rope_2d(rms_norm(q.reshape(b, s_len, AX_HEADS, HEAD_DIM), g_qn), cos_r, sin_r, cos_c, sin_c)
    k = rope_2d(rms_norm(k.reshape(b, s_len, AX_KV, HEAD_DIM), g_kn), cos_r, sin_r, cos_c, sin_c)
    v = v.reshape(b, s_len, AX_KV, HEAD_DIM)
    q = q.reshape(b, s_len, AX_KV, AX_HEADS // AX_KV, HEAD_DIM)
    return dense_attention_blocks(q, k, v, HEAD_DIM ** -0.5).reshape(b, s_len, AX_HEADS * HEAD_DIM)


def mixer_dilated(q, k, v, cos, sin):
    b, s_len, _ = q.shape
    q = apply_rope(q.reshape(b, s_len, DIL_HEADS, HEAD_DIM), cos, sin).reshape(b, s_len, DIL_KV, DIL_HEADS // DIL_KV, HEAD_DIM)
    k = apply_rope(k.reshape(b, s_len, DIL_KV, HEAD_DIM), cos, sin)
    v = v.reshape(b, s_len, DIL_KV, HEAD_DIM)
    return dilated_attention(q, k, v).reshape(b, s_len, DIL_HEADS * HEAD_DIM)


def mixer_mla(c_q, c_kv, k_r, g_cq, g_ckv, w_uq, w_ukv, cos, sin):
    b, s_len, _ = c_q.shape
    q = (rms_norm(c_q, g_cq) @ w_uq).reshape(b, s_len, MLA_HEADS, MLA_NOPE + MLA_ROPE)
    q = jnp.concatenate([q[..., :MLA_NOPE], apply_rope(q[..., MLA_NOPE:], cos, sin)], axis=-1)
    kv = (rms_norm(c_kv, g_ckv) @ w_ukv).reshape(b, s_len, MLA_HEADS, MLA_NOPE + MLA_V)
    k_r = apply_rope(k_r[:, :, None, :], cos, sin)
    k = jnp.concatenate([kv[..., :MLA_NOPE], jnp.broadcast_to(k_r, (b, s_len, MLA_HEADS, MLA_ROPE))], axis=-1)
    v = kv[..., MLA_NOPE:]
    o = dense_attention_blocks(q[:, :, :, None, :], k, v, (MLA_NOPE + MLA_ROPE) ** -0.5)
    return o.reshape(b, s_len, MLA_HEADS * MLA_V)


def peer_ffn(h, w_q, sub_keys, u_tab, v_tab):
    b, s_len, d = h.shape
    t = b * s_len
    hf = h.reshape(t, d)
    qry = (hf @ w_q).reshape(t, PEER_HEADS, 2, PEER_DK // 2)
    sc = jnp.einsum('thcd,hcnd->thcn', qry, sub_keys, preferred_element_type=jnp.float32)
    s1, i1 = lax.top_k(sc[:, :, 0], PEER_TOPK)
    s2, i2 = lax.top_k(sc[:, :, 1], PEER_TOPK)
    cand = (s1[..., :, None] + s2[..., None, :]).reshape(t, PEER_HEADS, PEER_TOPK * PEER_TOPK)
    cidx = (i1[..., :, None] * PEER_KEYS + i2[..., None, :]).reshape(t, PEER_HEADS, PEER_TOPK * PEER_TOPK)
    best, sel = lax.top_k(cand, PEER_TOPK)
    eidx = jnp.take_along_axis(cidx, sel, axis=-1)
    gate = jax.nn.softmax(best, axis=-1)
    nc = t // PEER_CHUNK

    def one(args):
        xc, ec, gc = args
        u = jnp.take(u_tab, ec, axis=0)
        a = jax.nn.gelu(jnp.einsum('chkd,cd->chk', u, xc), approximate=False)
        wgt = gc.astype(a.dtype) * a
        vv = jnp.take(v_tab, ec, axis=0)
        return jnp.einsum('chk,chkd->cd', wgt, vv)

    out = lax.map(one, (hf.reshape(nc, PEER_CHUNK, d),
                        eidx.reshape(nc, PEER_CHUNK, PEER_HEADS, PEER_TOPK),
                        gate.reshape(nc, PEER_CHUNK, PEER_HEADS, PEER_TOPK)))
    return out.reshape(b, s_len, d)


def setup_inputs(seed: int = 0) -> dict:
    key = jax.random.key(seed)
    ks = jax.random.split(key, 18)
    f32 = jnp.float32

    def nrm(k, shape, scale):
        return jax.random.normal(k, shape, f32) * scale

    def gain(k, shape):
        return 1.0 + 0.05 * jax.random.normal(k, shape, f32)

    return {
        'x': nrm(ks[0], (BATCH, SEQ, D_MODEL), 1.0),
        'g_mix': gain(ks[1], (DEPTH, D_MODEL)),
        'w_in': nrm(ks[2], (DEPTH, D_MODEL, IN_COLS), D_MODEL ** -0.5),
        'win_sink': nrm(ks[3], (DEPTH, WIN_HEADS), 0.5),
        'ax_q_norm': gain(ks[4], (DEPTH, HEAD_DIM)),
        'ax_k_norm': gain(ks[5], (DEPTH, HEAD_DIM)),
        'mla_q_norm': gain(ks[6], (DEPTH, MLA_Q_RANK)),
        'mla_kv_norm': gain(ks[7], (DEPTH, MLA_KV_RANK)),
        'mla_w_uq': nrm(ks[8], (DEPTH, MLA_Q_RANK, MLA_HEADS * (MLA_NOPE + MLA_ROPE)), MLA_Q_RANK ** -0.5),
        'mla_w_ukv': nrm(ks[9], (DEPTH, MLA_KV_RANK, MLA_HEADS * (MLA_NOPE + MLA_V)), MLA_KV_RANK ** -0.5),
        'g_group': gain(ks[10], (DEPTH, N_GROUPS, GROUP_W)),
        'w_out': nrm(ks[11], (DEPTH, MIX_WIDTH, D_MODEL), MIX_WIDTH ** -0.5),
        'g_ffn': gain(ks[12], (DEPTH, D_MODEL)),
        'peer_w_q': nrm(ks[13], (DEPTH, D_MODEL, PEER_HEADS * PEER_DK), D_MODEL ** -0.5),
        'peer_sub_keys': nrm(ks[14], (DEPTH, PEER_HEADS, 2, PEER_KEYS, PEER_DK // 2), (PEER_DK // 2) ** -0.5),
        'peer_u': nrm(ks[15], (DEPTH, PEER_N, D_MODEL), D_MODEL ** -0.5),
        'peer_v': nrm(ks[16], (DEPTH, PEER_N, D_MODEL), PEER_HEADS ** -0.5),
        'g_final': gain(ks[17], (D_MODEL,)),
    }


def reference(x, g_mix, w_in, win_sink, ax_q_norm, ax_k_norm, mla_q_norm, mla_kv_norm, mla_w_uq, mla_w_ukv,
              g_group, w_out, g_ffn, peer_w_q, peer_sub_keys, peer_u, peer_v, g_final):
    b, s_len, _ = x.shape
    pos = jnp.arange(s_len)
    rows = s_len // GRID_W
    row = jnp.repeat(jnp.arange(rows), GRID_W)
    col = jnp.tile(jnp.arange(GRID_W), rows)
    cos1, sin1 = rope_angles(pos, HEAD_DIM)
    cos_m, sin_m = rope_angles(pos, MLA_ROPE)
    cos_r, sin_r = rope_angles(row, HEAD_DIM // 2)
    cos_c, sin_c = rope_angles(col, HEAD_DIM // 2)
    splits = np.cumsum(IN_SIZES)[:-1].tolist()
    for l in range(DEPTH):
        h = rms_norm(x, g_mix[l])
        z = h @ w_in[l]
        (aq, ak, av, bq, bk, bv, cq, ck, cv, dcq, dckv, dkr) = jnp.split(z, splits, axis=-1)
        o_a = mixer_window(aq, ak, av, win_sink[l], cos1, sin1)
        o_b = mixer_axial(bq, bk, bv, ax_q_norm[l], ax_k_norm[l], cos_r, sin_r, cos_c, sin_c)
        o_c = mixer_dilated(cq, ck, cv, cos1, sin1)
        o_d = mixer_mla(dcq, dckv, dkr, mla_q_norm[l], mla_kv_norm[l], mla_w_uq[l], mla_w_ukv[l], cos_m, sin_m)
        y = jnp.concatenate([rms_norm(o_a, g_group[l, 0]), rms_norm(o_b, g_group[l, 1]),
                             rms_norm(o_c, g_group[l, 2]), rms_norm(o_d, g_group[l, 3])], axis=-1)
        x = x + y @ w_out[l]
        x = x + peer_ffn(rms_norm(x, g_ffn[l]), peer_w_q[l], peer_sub_keys[l], peer_u[l], peer_v[l])
    return rms_norm(x, g_final)
```

```python
import functools
import math

import jax
import jax.numpy as jnp
import numpy as np
from jax import lax
from jax.experimental import pallas as pl
from jax.experimental.pallas import tpu as pltpu

F32 = jnp.float32
BF16 = jnp.bfloat16

EPS = 1e-6
NEG = -1e30
ROPE_THETA = 10000.0
HEAD_DIM = 64
GRID_W = 64
WIN_RADIUS = 128
DIL_RADIUS = 64
DIL_STRIDES = (1, 4, 16)
MLA_NOPE = 64
MLA_ROPE = 32
MLA_PAD = 128
PEER_KEYS = 128
PEER_HEADS = 8
PEER_TOPK = 16
PEER_SLOTS = PEER_HEADS * PEER_TOPK
IN_PAD = 2048

LANES = 128
ROW_WORDS = 4
VMEM_LIMIT = 56 * 1024 * 1024


def _cparams(n_axes, vmem=None):
    return pltpu.CompilerParams(
        dimension_semantics=("arbitrary",) * n_axes,
        vmem_limit_bytes=vmem or VMEM_LIMIT)


def _rms(x, g):
    return x * lax.rsqrt(jnp.mean(x * x, axis=-1, keepdims=True) + EPS) * g


def _rope(x, tab_ref, shift):
    n = x.shape[-1]
    return (x * tab_ref[0] + pltpu.roll(x, n - shift, 1) * tab_ref[1]
            + pltpu.roll(x, shift, 1) * tab_ref[2])


def _prep_kernel(x_ref, g_ref, w_ref, taba_ref, tabb_ref, tabd_ref, gq_ref, gk_ref,
                 gcq_ref, gckv_ref, wuq_ref, wuk_ref, wuv_ref, hm_ref,
                 qa, ka, va, qb, kb, vb, qc, kc, vc, qd, kd, vd, z_scr):
    h = _rms(x_ref[...], g_ref[...])
    z_scr[...] = jnp.dot(h.astype(BF16), w_ref[...], preferred_element_type=F32)
    hd = HEAD_DIM
    qscale = hd ** -0.5

    def put_heads(dst, first, val):
        for i in range(val.shape[-1] // hd):
            dst[first + i] = val[:, i * hd:(i + 1) * hd].astype(dst.dtype)

    def rotary_group(base, q_dst, k_dst, v_dst):
        for half in range(2):
            q = _rope(z_scr[:, base + half * LANES: base + (half + 1) * LANES], taba_ref, hd // 2)
            put_heads(q_dst, 2 * half, q * qscale)
        put_heads(k_dst, 0, _rope(z_scr[:, base + 256: base + 384], taba_ref, hd // 2))
        put_heads(v_dst, 0, z_scr[:, base + 384: base + 512])

    rotary_group(0, qa, ka, va)
    rotary_group(1024, qc, kc, vc)

    def head_norm(x, g):
        ms = jnp.dot(x * x, hm_ref[...], preferred_element_type=F32)
        return x * lax.rsqrt(ms + EPS) * g

    for half in range(2):
        q = head_norm(z_scr[:, 512 + half * LANES: 512 + (half + 1) * LANES], gq_ref[...])
        put_heads(qb, 2 * half, _rope(q, tabb_ref, hd // 4) * qscale)
    k = head_norm(z_scr[:, 768:896], gk_ref[...])
    put_heads(kb, 0, _rope(k, tabb_ref, hd // 4))
    put_heads(vb, 0, z_scr[:, 896:1024])

    cq = _rms(z_scr[:, 1536:1792], gcq_ref[...]).astype(BF16)
    qfull = jnp.dot(cq, wuq_ref[...], preferred_element_type=F32)
    dscale = (MLA_NOPE + MLA_ROPE) ** -0.5
    ckv = _rms(z_scr[:, 1792:1920], gckv_ref[...]).astype(BF16)
    kfull = jnp.dot(ckv, wuk_ref[...], preferred_element_type=F32)
    vfull = jnp.dot(ckv, wuv_ref[...], preferred_element_type=F32)
    kr = _rope(z_scr[:, 1920:2048], tabd_ref, MLA_ROPE // 2)
    for i in range(4):
        sl = slice(i * MLA_PAD, (i + 1) * MLA_PAD)
        qd[i] = (_rope(qfull[:, sl], tabd_ref, MLA_ROPE // 2) * dscale).astype(qd.dtype)
        kd[i] = (kfull[:, sl] + kr).astype(kd.dtype)
    put_heads(vd, 0, vfull)


def _prep(x, g, w_pad, taba, tabb, tabd, gq, gk, gcq, gckv, wuq, wuk, wuv, hm, tm):
    b, s, d = x.shape
    nt = s // tm
    full = lambda shape: pl.BlockSpec(shape, lambda bi, i: (0,) * len(shape))
    tab = pl.BlockSpec((3, tm, LANES), lambda bi, i: (0, i, 0))

    def hspec(nh, w):
        return pl.BlockSpec((None, nh, tm, w), lambda bi, i: (bi, 0, i, 0))

    def hshape(nh, w):
        return jax.ShapeDtypeStruct((b, nh, s, w), BF16)

    trio_specs = [hspec(4, 64), hspec(2, 64), hspec(2, 64)]
    trio_shapes = [hshape(4, 64), hshape(2, 64), hshape(2, 64)]
    return pl.pallas_call(
        _prep_kernel,
        grid=(b, nt),
        in_specs=[pl.BlockSpec((None, tm, d), lambda bi, i: (bi, i, 0)),
                  full((1, d)), full((d, IN_PAD)), tab, tab, tab,
                  full((1, LANES)), full((1, LANES)), full((1, 256)), full((1, LANES)),
                  full((256, 4 * MLA_PAD)), full((LANES, 4 * MLA_PAD)), full((LANES, 256)),
                  full((LANES, LANES))],
        out_specs=trio_specs * 3 + [hspec(4, MLA_PAD), hspec(4, MLA_PAD), hspec(4, 64)],
        out_shape=trio_shapes * 3 + [hshape(4, MLA_PAD), hshape(4, MLA_PAD), hshape(4, 64)],
        scratch_shapes=[pltpu.VMEM((tm, IN_PAD), F32)],
        compiler_params=_cparams(2),
        name="prep",
    )(x, g, w_pad, taba, tabb, tabd, gq, gk, gcq, gckv, wuq, wuk, wuv, hm)


def _banded_kernel(*refs, radius, tq, win, seq, group, with_sink):
    if with_sink:
        sink_ref, q_ref, k_ref, v_ref, o_ref = refs
    else:
        q_ref, k_ref, v_ref, o_ref, lse_ref = refs
    q0 = pl.program_id(1) * tq
    start = pl.multiple_of(jnp.clip(q0 - radius, 0, seq - win), 64)
    qpos = q0 + lax.broadcasted_iota(jnp.int32, (tq, win), 0)
    kpos = start + lax.broadcasted_iota(jnp.int32, (tq, win), 1)
    valid = jnp.abs(kpos - qpos) <= radius
    hd = HEAD_DIM
    for h in range(q_ref.shape[0]):
        k = k_ref[h // group, pl.ds(start, win), :]
        v = v_ref[h // group, pl.ds(start, win), :]
        s = lax.dot_general(q_ref[h], k, (((1,), (1,)), ((), ())), preferred_element_type=F32)
        s = jnp.where(valid, s, NEG)
        m = jnp.max(s, axis=-1, keepdims=True)
        if with_sink:
            sink = sink_ref[h]
            m = jnp.maximum(m, sink)
        e = jnp.exp(s - m)
        den = jnp.sum(e, axis=-1, keepdims=True)
        if with_sink:
            den = den + jnp.exp(sink - m)
        o = jnp.dot(e.astype(BF16), v, preferred_element_type=F32) / den
        o_ref[:, h * hd:(h + 1) * hd] = o
        if not with_sink:
            lse_ref[:, h * hd:(h + 1) * hd] = jnp.broadcast_to(m + jnp.log(den), (tq, hd))


def _banded(q, k, v, radius, sink=None):
    nb, nh, seq, hd = q.shape
    nkv = k.shape[1]
    tq = min(256, seq)
    win = min(seq, tq + 2 * radius)
    with_sink = sink is not None
    kern = functools.partial(_banded_kernel, radius=radius, tq=tq, win=win, seq=seq,
                             group=nh // nkv, with_sink=with_sink)
    qspec = pl.BlockSpec((None, nh, tq, hd), lambda b, i: (b, 0, i, 0))
    kvspec = pl.BlockSpec((None, nkv, seq, hd), lambda b, i: (b, 0, 0, 0))
    ospec = pl.BlockSpec((None, tq, nh * hd), lambda b, i: (b, i, 0))
    oshape = jax.ShapeDtypeStruct((nb, seq, nh * hd), F32)
    in_specs = [qspec, kvspec, kvspec]
    args = [q, k, v]
    if with_sink:
        in_specs = [pl.BlockSpec(memory_space=pltpu.SMEM)] + in_specs
        args = [sink] + args
    return pl.pallas_call(
        kern,
        grid=(nb, seq // tq),
        in_specs=in_specs,
        out_specs=ospec if with_sink else [ospec, ospec],
        out_shape=oshape if with_sink else [oshape, oshape],
        compiler_params=_cparams(2),
        name="banded_sink" if with_sink else "banded_stats",
    )(*args)


def _dense_kernel(q_ref, k_ref, v_ref, o_ref, *, group):
    hd = v_ref.shape[-1]
    for h in range(q_ref.shape[0]):
        s = lax.dot_general(q_ref[h], k_ref[h // group], (((1,), (1,)), ((), ())),
                            preferred_element_type=F32)
        m = jnp.max(s, axis=-1, keepdims=True)
        e = jnp.exp(s - m)
        den = jnp.sum(e, axis=-1, keepdims=True)
        o = jnp.dot(e.astype(BF16), v_ref[h // group], preferred_element_type=F32)
        o_ref[:, h * hd:(h + 1) * hd] = o / den


def _dense(q, k, v, tq):
    b, nh, s, dk = q.shape
    nkv = k.shape[1]
    hd = v.shape[-1]
    return pl.pallas_call(
        functools.partial(_dense_kernel, group=nh // nkv),
        grid=(b, s // tq),
        in_specs=[pl.BlockSpec((None, nh, tq, dk), lambda bi, i: (bi, 0, i, 0)),
                  pl.BlockSpec((None, nkv, s, dk), lambda bi, i: (bi, 0, 0, 0)),
                  pl.BlockSpec((None, nkv, s, hd), lambda bi, i: (bi, 0, 0, 0))],
        out_specs=pl.BlockSpec((None, tq, nh * hd), lambda bi, i: (bi, i, 0)),
        out_shape=jax.ShapeDtypeStruct((b, s, nh * hd), F32),
        compiler_params=_cparams(2),
        name="dense_attn",
    )(q, k, v)


def _mix_out_kernel(x_ref, oa_ref, ob_ref, od_ref, o1_ref, l1_ref, o4_ref, l4_ref, o16_ref, l16_ref,
                    gg_ref, w_ref, out_ref):
    l1, l4, l16 = l1_ref[...], l4_ref[...], l16_ref[...]
    lm = jnp.maximum(jnp.maximum(l1, l4), l16)
    w1, w4, w16 = jnp.exp(l1 - lm), jnp.exp(l4 - lm), jnp.exp(l16 - lm)
    oc = (w1 * o1_ref[...] + w4 * o4_ref[...] + w16 * o16_ref[...]) / (w1 + w4 + w16)
    gg = gg_ref[...]
    y = jnp.concatenate([_rms(oa_ref[...], gg[0:1]), _rms(ob_ref[...], gg[1:2]),
                         _rms(oc, gg[2:3]), _rms(od_ref[...], gg[3:4])], axis=-1)
    out_ref[...] = x_ref[...] + jnp.dot(y.astype(BF16), w_ref[...], preferred_element_type=F32)


def _mix_out(x, oa, ob, od, o1, l1, o4, l4, o16, l16, gg, w, tm):
    t, d = x.shape
    row = lambda w_: pl.BlockSpec((tm, w_), lambda i: (i, 0))
    full = lambda shape: pl.BlockSpec(shape, lambda i: (0,) * len(shape))
    return pl.pallas_call(
        _mix_out_kernel,
        grid=(t // tm,),
        in_specs=[row(d)] + [row(256)] * 9 + [full(gg.shape), full(w.shape)],
        out_specs=row(d),
        out_shape=jax.ShapeDtypeStruct((t, d), F32),
        compiler_params=_cparams(1),
        name="mix_out",
    )(x, oa, ob, od, o1, l1, o4, l4, o16, l16, gg, w)


def _topk_rows(vals, payload, k):
    rows = lax.broadcasted_iota(jnp.int32, vals.shape, 0)
    best, pay = [], []
    for _ in range(k):
        m = jnp.max(vals, axis=0, keepdims=True)
        idx = jnp.min(jnp.where(vals == m, rows, vals.shape[0]), axis=0, keepdims=True)
        hit = rows == idx
        best.append(m)
        pay.append(jnp.max(jnp.where(hit, payload, -1), axis=0, keepdims=True))
        vals = jnp.where(hit, -jnp.inf, vals)
    return jnp.concatenate(best, axis=0), jnp.concatenate(pay, axis=0)


_PAIR_LIMIT = tuple(PEER_TOPK // (i + 1) for i in range(PEER_TOPK))


def _peer_sel_kernel(x_ref, g_ref, wq_ref, keys_ref, hn_ref, eid_ref, gate_ref, q_scr):
    hn = _rms(x_ref[...], g_ref[...])
    hn_ref[...] = hn
    q_scr[...] = jnp.dot(hn.astype(BF16), wq_ref[...], preferred_element_type=F32).astype(BF16)
    tm = x_ref.shape[0]
    key_ids = lax.broadcasted_iota(jnp.int32, (PEER_KEYS, tm), 0)
    row8 = lax.broadcasted_iota(jnp.int32, (8, tm), 0)
    for h in range(PEER_HEADS):
        tops = []
        for c in range(2):
            col = (2 * h + c) * LANES
            sc = lax.dot_general(keys_ref[h, c], q_scr[:, col:col + LANES],
                                 (((1,), (1,)), ((), ())), preferred_element_type=F32)
            tops.append(_topk_rows(sc, key_ids, PEER_TOPK))
        (s1, i1), (s2, i2) = tops
        cand, cid = [s1[0:1] + s2], [i1[0:1] * PEER_KEYS + i2]
        for i in range(1, 8):
            ok = row8 < _PAIR_LIMIT[i]
            cand.append(jnp.where(ok, s1[i:i + 1] + s2[0:8], -jnp.inf))
            cid.append(i1[i:i + 1] * PEER_KEYS + i2[0:8])
        cand.append(s1[8:16] + s2[0:1])
        cid.append(i1[8:16] * PEER_KEYS + i2[0:1])
        best, eid = _topk_rows(jnp.concatenate(cand, axis=0), jnp.concatenate(cid, axis=0), PEER_TOPK)
        e = jnp.exp(best - best[0:1])
        gate = e / jnp.sum(e, axis=0, keepdims=True)
        eid_ref[h * PEER_TOPK:(h + 1) * PEER_TOPK, :] = eid
        gate_ref[h * PEER_TOPK:(h + 1) * PEER_TOPK, :] = gate


def _peer_sel(x, g, wq, keys, tm):
    t, d = x.shape
    full = lambda shape: pl.BlockSpec(shape, lambda i: (0,) * len(shape))
    return pl.pallas_call(
        _peer_sel_kernel,
        grid=(t // tm,),
        in_specs=[pl.BlockSpec((tm, d), lambda i: (i, 0)), full((1, d)), full(wq.shape), full(keys.shape)],
        out_specs=[pl.BlockSpec((tm, d), lambda i: (i, 0)),
                   pl.BlockSpec((PEER_SLOTS, tm), lambda i: (0, i)),
                   pl.BlockSpec((PEER_SLOTS, tm), lambda i: (0, i))],
        out_shape=[jax.ShapeDtypeStruct((t, d), F32),
                   jax.ShapeDtypeStruct((PEER_SLOTS, t), jnp.int32),
                   jax.ShapeDtypeStruct((PEER_SLOTS, t), F32)],
        scratch_shapes=[pltpu.VMEM((tm, wq.shape[1]), BF16)],
        compiler_params=_cparams(1),
        name="peer_sel",
    )(x, g, wq, keys)


def _pack_table(tab):
    n, d = tab.shape
    bits = lax.bitcast_convert_type(tab.astype(BF16), jnp.uint16).astype(jnp.uint32)
    words = bits[:, :d // 2] | (bits[:, d // 2:] << 16)
    return words.reshape(n * ROW_WORDS, LANES)


def _unpack(words):
    lo = pltpu.bitcast(words << 16, F32)
    hi = pltpu.bitcast(words & jnp.uint32(0xFFFF0000), F32)
    return lo, hi


def _peer_u_kernel(eid_ref, tab_ref, hn_ref, gate_ref, w_ref, p_scr, a_scr):
    tm = hn_ref.shape[0]
    ones = jnp.ones((LANES, LANES), BF16)
    rows = lax.broadcasted_iota(jnp.int32, (PEER_SLOTS * ROW_WORDS, LANES), 0)
    lanes = lax.broadcasted_iota(jnp.int32, (PEER_SLOTS * ROW_WORDS, LANES), 1)
    diag = (rows // ROW_WORDS) == lanes

    def token(t, carry):
        hv = hn_ref[t]
        h_lo, h_hi = hv[0:ROW_WORDS], hv[ROW_WORDS:2 * ROW_WORDS]
        for j in range(PEER_SLOTS):
            e = eid_ref[j, t]
            lo, hi = _unpack(tab_ref[pl.ds(pl.multiple_of(e * ROW_WORDS, ROW_WORDS), ROW_WORDS), :])
            p_scr[j * ROW_WORDS:(j + 1) * ROW_WORDS, :] = lo * h_lo + hi * h_hi
        part = jnp.dot(p_scr[...].astype(BF16), ones, preferred_element_type=F32)
        a_scr[pl.ds(t, 1), :] = jnp.sum(jnp.where(diag, part, 0.0), axis=0, keepdims=True)
        return carry

    lax.fori_loop(0, tm, token, 0)
    a = a_scr[...]
    act = 0.5 * a * (1.0 + lax.erf(a * (2.0 ** -0.5)))
    w_ref[...] = gate_ref[...].T * act


def _peer_u(eid_t, table, hn3, gate_t, tm):
    t = hn3.shape[0]
    return pl.pallas_call(
        _peer_u_kernel,
        grid=(t // tm,),
        in_specs=[pl.BlockSpec((PEER_SLOTS, tm), lambda i: (0, i), memory_space=pltpu.SMEM),
                  pl.BlockSpec(memory_space=pltpu.VMEM),
                  pl.BlockSpec((tm, 8, LANES), lambda i: (i, 0, 0)),
                  pl.BlockSpec((PEER_SLOTS, tm), lambda i: (0, i))],
        out_specs=pl.BlockSpec((tm, PEER_SLOTS), lambda i: (i, 0)),
        out_shape=jax.ShapeDtypeStruct((t, PEER_SLOTS), F32),
        scratch_shapes=[pltpu.VMEM((PEER_SLOTS * ROW_WORDS, LANES), F32),
                        pltpu.VMEM((tm, PEER_SLOTS), F32)],
        compiler_params=_cparams(1),
        name="peer_u",
    )(eid_t, table, hn3, gate_t)


def _peer_v_kernel(eid_ref, w_ref, tab_ref, x_ref, out_ref):
    tm = x_ref.shape[0]

    def token(t, carry):
        acc = [jnp.zeros((ROW_WORDS, LANES), F32) for _ in range(4)]
        for j in range(PEER_SLOTS):
            e = eid_ref[j, t]
            wj = w_ref[t, j]
            lo, hi = _unpack(tab_ref[pl.ds(pl.multiple_of(e * ROW_WORDS, ROW_WORDS), ROW_WORDS), :])
            par = 2 * (j % 2)
            acc[par] = acc[par] + wj * lo
            acc[par + 1] = acc[par + 1] + wj * hi
        out_ref[t] = x_ref[t] + jnp.concatenate([acc[0] + acc[2], acc[1] + acc[3]], axis=0)
        return carry

    lax.fori_loop(0, tm, token, 0)


def _peer_v(eid_t, w, table, x3, tm):
    t = x3.shape[0]
    return pl.pallas_call(
        _peer_v_kernel,
        grid=(t // tm,),
        in_specs=[pl.BlockSpec((PEER_SLOTS, tm), lambda i: (0, i), memory_space=pltpu.SMEM),
                  pl.BlockSpec((tm, PEER_SLOTS), lambda i: (i, 0), memory_space=pltpu.SMEM),
                  pl.BlockSpec(memory_space=pltpu.VMEM),
                  pl.BlockSpec((tm, 8, LANES), lambda i: (i, 0, 0))],
        out_specs=pl.BlockSpec((tm, 8, LANES), lambda i: (i, 0, 0)),
        out_shape=jax.ShapeDtypeStruct(x3.shape, F32),
        compiler_params=_cparams(1),
        name="peer_v",
    )(eid_t, w, table, x3)


def _final_norm_kernel(x_ref, g_ref, o_ref):
    o_ref[...] = _rms(x_ref[...], g_ref[...])


def _final_norm(x, g, tm):
    t, d = x.shape
    return pl.pallas_call(
        _final_norm_kernel,
        grid=(t // tm,),
        in_specs=[pl.BlockSpec((tm, d), lambda i: (i, 0)), pl.BlockSpec((1, d), lambda i: (0, 0))],
        out_specs=pl.BlockSpec((tm, d), lambda i: (i, 0)),
        out_shape=jax.ShapeDtypeStruct((t, d), F32),
        compiler_params=_cparams(1),
        name="final_norm",
    )(x, g)


def _rope_angles(pos, dim):
    inv = ROPE_THETA ** (-jnp.arange(0, dim, 2, dtype=F32) / dim)
    ang = pos.astype(F32)[:, None] * inv[None, :]
    return jnp.cos(ang), jnp.sin(ang)


def _rope_table(parts, s_len):
    cs, sn, sp = [], [], []
    for width, cos, sin, half in parts:
        if cos is None:
            cs.append(jnp.ones((s_len, width), F32))
            sn.append(jnp.zeros((s_len, width), F32))
            sp.append(jnp.zeros((s_len, width), F32))
            continue
        reps = width // (2 * half)
        zero = jnp.zeros_like(sin)
        cs.append(jnp.tile(jnp.concatenate([cos, cos], axis=-1), (1, reps)))
        sn.append(jnp.tile(jnp.concatenate([-sin, zero], axis=-1), (1, reps)))
        sp.append(jnp.tile(jnp.concatenate([zero, sin], axis=-1), (1, reps)))
    return jnp.stack([jnp.concatenate(cs, -1), jnp.concatenate(sn, -1), jnp.concatenate(sp, -1)])


def _deinterleave(a, stride):
    if stride == 1:
        return a
    b, h, s, d = a.shape
    return a.reshape(b, h, s // stride, stride, d).transpose(0, 3, 1, 2, 4).reshape(b * stride, h, s // stride, d)


def _reinterleave(a, stride, b):
    if stride == 1:
        return a
    _, l, w = a.shape
    return a.reshape(b, stride, l, w).transpose(0, 2, 1, 3).reshape(b, l * stride, w)


def kernel(x, g_mix, w_in, win_sink, ax_q_norm, ax_k_norm, mla_q_norm, mla_kv_norm, mla_w_uq, mla_w_ukv,
           g_group, w_out, g_ffn, peer_w_q, peer_sub_keys, peer_u, peer_v, g_final):
    b, s, d = x.shape
    t = b * s
    depth = w_in.shape[0]
    hd = HEAD_DIM

    pos = jnp.arange(s)
    cos1, sin1 = _rope_angles(pos, hd)
    cos_m, sin_m = _rope_angles(pos, MLA_ROPE)
    cos_r, sin_r = _rope_angles(pos // GRID_W, hd // 2)
    cos_c, sin_c = _rope_angles(pos % GRID_W, hd // 2)
    taba = _rope_table([(LANES, cos1, sin1, hd // 2)], s)
    tabb = _rope_table([(32, cos_r, sin_r, 16), (32, cos_c, sin_c, 16)] * 2, s)
    tabd = _rope_table([(MLA_NOPE, None, None, 0), (MLA_ROPE, cos_m, sin_m, 16), (32, None, None, 0)], s)
    head_mean = jnp.asarray(np.kron(np.eye(LANES // hd), np.full((hd, hd), 1.0 / hd)), F32)

    tm_prep = min(256, s)
    tm_tok = 256
    tm_gather = 128
    xf = x
    for l in range(depth):
        w_l = w_in[l]
        kr_cols = jnp.concatenate([jnp.zeros((d, MLA_NOPE), F32), w_l[:, 1920:1952], jnp.zeros((d, 32), F32)], -1)
        w_pad = jnp.concatenate([w_l[:, :1920], kr_cols], axis=-1).astype(BF16)
        wuq = mla_w_uq[l].reshape(-1, 4, MLA_NOPE + MLA_ROPE)
        wuq = jnp.pad(wuq, ((0, 0), (0, 0), (0, MLA_PAD - MLA_NOPE - MLA_ROPE))).reshape(-1, 4 * MLA_PAD).astype(BF16)
        wukv = mla_w_ukv[l].reshape(-1, 4, MLA_NOPE + hd)
        wuk = jnp.pad(wukv[:, :, :MLA_NOPE], ((0, 0), (0, 0), (0, MLA_PAD - MLA_NOPE))).reshape(-1, 4 * MLA_PAD).astype(BF16)
        wuv = wukv[:, :, MLA_NOPE:].reshape(-1, 4 * hd).astype(BF16)
        gq = jnp.tile(ax_q_norm[l], 2)[None]
        gk = jnp.tile(ax_k_norm[l], 2)[None]

        (qa, ka, va, qb, kb, vb, qc, kc, vc, qd, kd, vd) = _prep(
            xf.reshape(b, s, d), g_mix[l][None], w_pad, taba, tabb, tabd, gq, gk,
            mla_q_norm[l][None], mla_kv_norm[l][None], wuq, wuk, wuv, head_mean, tm_prep)

        oa = _banded(qa, ka, va, WIN_RADIUS, sink=win_sink[l])
        ob = _dense(qb, kb, vb, tm_prep)
        od = _dense(qd, kd, vd, tm_prep)
        dil = []
        for stride in DIL_STRIDES:
            o, lse = _banded(_deinterleave(qc, stride), _deinterleave(kc, stride), _deinterleave(vc, stride),
                             DIL_RADIUS)
            dil += [_reinterleave(o, stride, b).reshape(t, 4 * hd), _reinterleave(lse, stride, b).reshape(t, 4 * hd)]

        xf = _mix_out(xf.reshape(t, d), oa.reshape(t, 4 * hd), ob.reshape(t, 4 * hd), od.reshape(t, 4 * hd),
                      *dil, g_group[l], w_out[l].astype(BF16), tm_tok)

        hn, eid_t, gate_t = _peer_sel(xf, g_ffn[l][None], peer_w_q[l].astype(BF16),
                                      peer_sub_keys[l].astype(BF16), tm_tok)
        wgt = _peer_u(eid_t, _pack_table(peer_u[l]), hn.reshape(t, 8, LANES), gate_t, tm_gather)
        xf = _peer_v(eid_t, wgt, _pack_table(peer_v[l]), xf.reshape(t, 8, LANES), tm_gather).reshape(t, d)

    return _final_norm(xf, g_final[None], tm_tok).reshape(b, s, d)
```

```python
import functools
import math

import jax
import jax.numpy as jnp
import numpy as np
from jax import lax
from jax.experimental import pallas as pl
from jax.experimental.pallas import tpu as pltpu

F32 = jnp.float32
BF16 = jnp.bfloat16

EPS = 1e-6
NEG = -1e30
ROPE_THETA = 10000.0
HEAD_DIM = 64
GRID_W = 64
WIN_RADIUS = 128
DIL_RADIUS = 64
DIL_STRIDES = (1, 4, 16)
MLA_NOPE = 64
MLA_ROPE = 32
MLA_PAD = 128
PEER_KEYS = 128
PEER_HEADS = 8
PEER_TOPK = 16
PEER_SLOTS = PEER_HEADS * PEER_TOPK
IN_PAD = 2048

LANES = 128
ROW_WORDS = 4
GATHER_UNROLL = 8
VMEM_LIMIT = 56 * 1024 * 1024


def _cparams(n_axes, vmem=None):
    return pltpu.CompilerParams(
        dimension_semantics=("arbitrary",) * n_axes,
        vmem_limit_bytes=vmem or VMEM_LIMIT)


def _rms(x, g):
    return x * lax.rsqrt(jnp.mean(x * x, axis=-1, keepdims=True) + EPS) * g


def _rope(x, tab_ref, shift):
    n = x.shape[-1]
    return (x * tab_ref[0] + pltpu.roll(x, n - shift, 1) * tab_ref[1]
            + pltpu.roll(x, shift, 1) * tab_ref[2])


def _prep_kernel(x_ref, g_ref, w_ref, taba_ref, tabb_ref, tabd_ref, gq_ref, gk_ref,
                 gcq_ref, gckv_ref, wuq_ref, wuk_ref, wuv_ref, hm_ref,
                 qa, ka, va, qb, kb, vb, qc, kc, vc, qd, kd, vd, z_scr):
    h = _rms(x_ref[...], g_ref[...])
    z_scr[...] = jnp.dot(h.astype(BF16), w_ref[...], preferred_element_type=F32)
    hd = HEAD_DIM
    qscale = hd ** -0.5

    def put_heads(dst, first, val):
        for i in range(val.shape[-1] // hd):
            dst[first + i] = val[:, i * hd:(i + 1) * hd].astype(dst.dtype)

    def rotary_group(base, q_dst, k_dst, v_dst):
        for half in range(2):
            q = _rope(z_scr[:, base + half * LANES: base + (half + 1) * LANES], taba_ref, hd // 2)
            put_heads(q_dst, 2 * half, q * qscale)
        put_heads(k_dst, 0, _rope(z_scr[:, base + 256: base + 384], taba_ref, hd // 2))
        put_heads(v_dst, 0, z_scr[:, base + 384: base + 512])

    rotary_group(0, qa, ka, va)
    rotary_group(1024, qc, kc, vc)

    def head_norm(x, g):
        ms = jnp.dot(x * x, hm_ref[...], preferred_element_type=F32)
        return x * lax.rsqrt(ms + EPS) * g

    for half in range(2):
        q = head_norm(z_scr[:, 512 + half * LANES: 512 + (half + 1) * LANES], gq_ref[...])
        put_heads(qb, 2 * half, _rope(q, tabb_ref, hd // 4) * qscale)
    k = head_norm(z_scr[:, 768:896], gk_ref[...])
    put_heads(kb, 0, _rope(k, tabb_ref, hd // 4))
    put_heads(vb, 0, z_scr[:, 896:1024])

    cq = _rms(z_scr[:, 1536:1792], gcq_ref[...]).astype(BF16)
    qfull = jnp.dot(cq, wuq_ref[...], preferred_element_type=F32)
    dscale = (MLA_NOPE + MLA_ROPE) ** -0.5
    ckv = _rms(z_scr[:, 1792:1920], gckv_ref[...]).astype(BF16)
    kfull = jnp.dot(ckv, wuk_ref[...], preferred_element_type=F32)
    vfull = jnp.dot(ckv, wuv_ref[...], preferred_element_type=F32)
    kr = _rope(z_scr[:, 1920:2048], tabd_ref, MLA_ROPE // 2)
    for i in range(4):
        sl = slice(i * MLA_PAD, (i + 1) * MLA_PAD)
        qd[i] = (_rope(qfull[:, sl], tabd_ref, MLA_ROPE // 2) * dscale).astype(qd.dtype)
        kd[i] = (kfull[:, sl] + kr).astype(kd.dtype)
    put_heads(vd, 0, vfull)


def _prep(x, g, w_pad, taba, tabb, tabd, gq, gk, gcq, gckv, wuq, wuk, wuv, hm, tm):
    b, s, d = x.shape
    nt = s // tm
    full = lambda shape: pl.BlockSpec(shape, lambda bi, i: (0,) * len(shape))
    tab = pl.BlockSpec((3, tm, LANES), lambda bi, i: (0, i, 0))

    def hspec(nh, w):
        return pl.BlockSpec((None, nh, tm, w), lambda bi, i: (bi, 0, i, 0))

    def hshape(nh, w):
        return jax.ShapeDtypeStruct((b, nh, s, w), BF16)

    trio_specs = [hspec(4, 64), hspec(2, 64), hspec(2, 64)]
    trio_shapes = [hshape(4, 64), hshape(2, 64), hshape(2, 64)]
    return pl.pallas_call(
        _prep_kernel,
        grid=(b, nt),
        in_specs=[pl.BlockSpec((None, tm, d), lambda bi, i: (bi, i, 0)),
                  full((1, d)), full((d, IN_PAD)), tab, tab, tab,
                  full((1, LANES)), full((1, LANES)), full((1, 256)), full((1, LANES)),
                  full((256, 4 * MLA_PAD)), full((LANES, 4 * MLA_PAD)), full((LANES, 256)),
                  full((LANES, LANES))],
        out_specs=trio_specs * 3 + [hspec(4, MLA_PAD), hspec(4, MLA_PAD), hspec(4, 64)],
        out_shape=trio_shapes * 3 + [hshape(4, MLA_PAD), hshape(4, MLA_PAD), hshape(4, 64)],
        scratch_shapes=[pltpu.VMEM((tm, IN_PAD), F32)],
        compiler_params=_cparams(2),
        name="prep",
    )(x, g, w_pad, taba, tabb, tabd, gq, gk, gcq, gckv, wuq, wuk, wuv, hm)


def _banded_kernel(*refs, radius, tq, win, seq, group, with_sink):
    if with_sink:
        sink_ref, q_ref, k_ref, v_ref, o_ref = refs
    else:
        q_ref, k_ref, v_ref, o_ref, lse_ref = refs
    q0 = pl.program_id(1) * tq
    start = pl.multiple_of(jnp.clip(q0 - radius, 0, seq - win), 64)
    qpos = q0 + lax.broadcasted_iota(jnp.int32, (tq, win), 0)
    kpos = start + lax.broadcasted_iota(jnp.int32, (tq, win), 1)
    valid = jnp.abs(kpos - qpos) <= radius
    hd = HEAD_DIM
    for h in range(q_ref.shape[0]):
        k = k_ref[h // group, pl.ds(start, win), :]
        v = v_ref[h // group, pl.ds(start, win), :]
        s = lax.dot_general(q_ref[h], k, (((1,), (1,)), ((), ())), preferred_element_type=F32)
        s = jnp.where(valid, s, NEG)
        m = jnp.max(s, axis=-1, keepdims=True)
        if with_sink:
            sink = sink_ref[h]
            m = jnp.maximum(m, sink)
        e = jnp.exp(s - m)
        den = jnp.sum(e, axis=-1, keepdims=True)
        if with_sink:
            den = den + jnp.exp(sink - m)
        o = jnp.dot(e.astype(BF16), v, preferred_element_type=F32) / den
        o_ref[:, h * hd:(h + 1) * hd] = o
        if not with_sink:
            lse_ref[:, h * hd:(h + 1) * hd] = jnp.broadcast_to(m + jnp.log(den), (tq, hd))


def _banded(q, k, v, radius, sink=None):
    nb, nh, seq, hd = q.shape
    nkv = k.shape[1]
    tq = min(256, seq)
    win = min(seq, tq + 2 * radius)
    with_sink = sink is not None
    kern = functools.partial(_banded_kernel, radius=radius, tq=tq, win=win, seq=seq,
                             group=nh // nkv, with_sink=with_sink)
    qspec = pl.BlockSpec((None, nh, tq, hd), lambda b, i: (b, 0, i, 0))
    kvspec = pl.BlockSpec((None, nkv, seq, hd), lambda b, i: (b, 0, 0, 0))
    ospec = pl.BlockSpec((None, tq, nh * hd), lambda b, i: (b, i, 0))
    oshape = jax.ShapeDtypeStruct((nb, seq, nh * hd), F32)
    in_specs = [qspec, kvspec, kvspec]
    args = [q, k, v]
    if with_sink:
        in_specs = [pl.BlockSpec(memory_space=pltpu.SMEM)] + in_specs
        args = [sink] + args
    return pl.pallas_call(
        kern,
        grid=(nb, seq // tq),
        in_specs=in_specs,
        out_specs=ospec if with_sink else [ospec, ospec],
        out_shape=oshape if with_sink else [oshape, oshape],
        compiler_params=_cparams(2),
        name="banded_sink" if with_sink else "banded_stats",
    )(*args)


def _dense_kernel(q_ref, k_ref, v_ref, o_ref, *, group):
    hd = v_ref.shape[-1]
    for h in range(q_ref.shape[0]):
        s = lax.dot_general(q_ref[h], k_ref[h // group], (((1,), (1,)), ((), ())),
                            preferred_element_type=F32)
        m = jnp.max(s, axis=-1, keepdims=True)
        e = jnp.exp(s - m)
        den = jnp.sum(e, axis=-1, keepdims=True)
        o = jnp.dot(e.astype(BF16), v_ref[h // group], preferred_element_type=F32)
        o_ref[:, h * hd:(h + 1) * hd] = o / den


def _dense(q, k, v, tq):
    b, nh, s, dk = q.shape
    nkv = k.shape[1]
    hd = v.shape[-1]
    return pl.pallas_call(
        functools.partial(_dense_kernel, group=nh // nkv),
        grid=(b, s // tq),
        in_specs=[pl.BlockSpec((None, nh, tq, dk), lambda bi, i: (bi, 0, i, 0)),
                  pl.BlockSpec((None, nkv, s, dk), lambda bi, i: (bi, 0, 0, 0)),
                  pl.BlockSpec((None, nkv, s, hd), lambda bi, i: (bi, 0, 0, 0))],
        out_specs=pl.BlockSpec((None, tq, nh * hd), lambda bi, i: (bi, i, 0)),
        out_shape=jax.ShapeDtypeStruct((b, s, nh * hd), F32),
        compiler_params=_cparams(2),
        name="dense_attn",
    )(q, k, v)


def _mix_out_kernel(x_ref, oa_ref, ob_ref, od_ref, o1_ref, l1_ref, o4_ref, l4_ref, o16_ref, l16_ref,
                    gg_ref, w_ref, out_ref):
    l1, l4, l16 = l1_ref[...], l4_ref[...], l16_ref[...]
    lm = jnp.maximum(jnp.maximum(l1, l4), l16)
    w1, w4, w16 = jnp.exp(l1 - lm), jnp.exp(l4 - lm), jnp.exp(l16 - lm)
    oc = (w1 * o1_ref[...] + w4 * o4_ref[...] + w16 * o16_ref[...]) / (w1 + w4 + w16)
    gg = gg_ref[...]
    y = jnp.concatenate([_rms(oa_ref[...], gg[0:1]), _rms(ob_ref[...], gg[1:2]),
                         _rms(oc, gg[2:3]), _rms(od_ref[...], gg[3:4])], axis=-1)
    out_ref[...] = x_ref[...] + jnp.dot(y.astype(BF16), w_ref[...], preferred_element_type=F32)


def _mix_out(x, oa, ob, od, o1, l1, o4, l4, o16, l16, gg, w, tm):
    t, d = x.shape
    row = lambda w_: pl.BlockSpec((tm, w_), lambda i: (i, 0))
    full = lambda shape: pl.BlockSpec(shape, lambda i: (0,) * len(shape))
    return pl.pallas_call(
        _mix_out_kernel,
        grid=(t // tm,),
        in_specs=[row(d)] + [row(256)] * 9 + [full(gg.shape), full(w.shape)],
        out_specs=row(d),
        out_shape=jax.ShapeDtypeStruct((t, d), F32),
        compiler_params=_cparams(1),
        name="mix_out",
    )(x, oa, ob, od, o1, l1, o4, l4, o16, l16, gg, w)


def _topk_rows(vals, payload, k):
    rows = lax.broadcasted_iota(jnp.int32, vals.shape, 0)
    best, pay = [], []
    for _ in range(k):
        m = jnp.max(vals, axis=0, keepdims=True)
        idx = jnp.min(jnp.where(vals == m, rows, vals.shape[0]), axis=0, keepdims=True)
        hit = rows == idx
        best.append(m)
        pay.append(jnp.max(jnp.where(hit, payload, -1), axis=0, keepdims=True))
        vals = jnp.where(hit, -jnp.inf, vals)
    return jnp.concatenate(best, axis=0), jnp.concatenate(pay, axis=0)


_PAIR_LIMIT = tuple(PEER_TOPK // (i + 1) for i in range(PEER_TOPK))


def _peer_sel_kernel(x_ref, g_ref, wq_ref, keys_ref, hn_ref, eid_ref, gate_ref, q_scr):
    hn = _rms(x_ref[...], g_ref[...])
    hn_ref[...] = hn
    q_scr[...] = jnp.dot(hn.astype(BF16), wq_ref[...], preferred_element_type=F32).astype(BF16)
    tm = x_ref.shape[0]
    key_ids = lax.broadcasted_iota(jnp.int32, (PEER_KEYS, tm), 0)
    row8 = lax.broadcasted_iota(jnp.int32, (8, tm), 0)
    eids, gates = [], []
    for h in range(PEER_HEADS):
        tops = []
        for c in range(2):
            col = (2 * h + c) * LANES
            sc = lax.dot_general(keys_ref[h, c], q_scr[:, col:col + LANES],
                                 (((1,), (1,)), ((), ())), preferred_element_type=F32)
            tops.append(_topk_rows(sc, key_ids, PEER_TOPK))
        (s1, i1), (s2, i2) = tops
        i1, i2 = i1 * ROW_WORDS, i2 * ROW_WORDS
        cand, cid = [s1[0:1] + s2], [i1[0:1] * PEER_KEYS + i2]
        for i in range(1, 8):
            ok = row8 < _PAIR_LIMIT[i]
            cand.append(jnp.where(ok, s1[i:i + 1] + s2[0:8], -jnp.inf))
            cid.append(i1[i:i + 1] * PEER_KEYS + i2[0:8])
        cand.append(s1[8:16] + s2[0:1])
        cid.append(i1[8:16] * PEER_KEYS + i2[0:1])
        best, eid = _topk_rows(jnp.concatenate(cand, axis=0), jnp.concatenate(cid, axis=0), PEER_TOPK)
        e = jnp.exp(best - best[0:1])
        gate = e / jnp.sum(e, axis=0, keepdims=True)
        eids.append(eid)
        gates.append(gate)
    eid_ref[...] = jnp.concatenate(eids, axis=0).T
    gate_ref[...] = jnp.concatenate(gates, axis=0).T


def _peer_sel(x, g, wq, keys, tm):
    t, d = x.shape
    full = lambda shape: pl.BlockSpec(shape, lambda i: (0,) * len(shape))
    return pl.pallas_call(
        _peer_sel_kernel,
        grid=(t // tm,),
        in_specs=[pl.BlockSpec((tm, d), lambda i: (i, 0)), full((1, d)), full(wq.shape), full(keys.shape)],
        out_specs=[pl.BlockSpec((tm, d), lambda i: (i, 0)),
                   pl.BlockSpec((tm, PEER_SLOTS), lambda i: (i, 0)),
                   pl.BlockSpec((tm, PEER_SLOTS), lambda i: (i, 0))],
        out_shape=[jax.ShapeDtypeStruct((t, d), F32),
                   jax.ShapeDtypeStruct((t, PEER_SLOTS), jnp.int32),
                   jax.ShapeDtypeStruct((t, PEER_SLOTS), F32)],
        scratch_shapes=[pltpu.VMEM((tm, wq.shape[1]), BF16)],
        compiler_params=_cparams(1),
        name="peer_sel",
    )(x, g, wq, keys)


def _pack_table(tab):
    n, d = tab.shape
    bits = lax.bitcast_convert_type(tab.astype(BF16), jnp.uint16).astype(jnp.uint32)
    words = bits[:, :d // 2] | (bits[:, d // 2:] << 16)
    return words.reshape(n * ROW_WORDS, LANES)


def _slot_masks():
    col = lax.broadcasted_iota(jnp.int32, (8, PEER_SLOTS * 8), 1) % 8
    row = lax.broadcasted_iota(jnp.int32, (8, PEER_SLOTS * 8), 0)
    return col == 2 * (row % ROW_WORDS) + row // ROW_WORDS


def _split_bf16(x):
    hi = x.astype(BF16)
    return hi, (x - hi.astype(F32)).astype(BF16)


def _gather_rows(eid_ref, tab_ref, g_scr, t):
    for j in range(PEER_SLOTS):
        first_row = pl.multiple_of(eid_ref[t, j], ROW_WORDS)
        g_scr[j * ROW_WORDS:(j + 1) * ROW_WORDS, :] = tab_ref[pl.ds(first_row, ROW_WORDS), :]
    return pltpu.bitcast(g_scr[...], BF16)


def _token_loop(tm, g_scr, token):
    def trip(i, carry):
        for u in range(GATHER_UNROLL):
            token(i * GATHER_UNROLL + u, g_scr.at[u])
        return carry

    lax.fori_loop(0, tm // GATHER_UNROLL, trip, 0)


def _peer_u_kernel(eid_ref, tab_ref, hn_ref, gate_ref, w_ref, g_scr, r_scr):
    tm = hn_ref.shape[0]
    own = _slot_masks()

    def token(t, g_tile):
        rows = _gather_rows(eid_ref, tab_ref, g_tile, t)
        h_hi, h_lo = _split_bf16(hn_ref[t])
        d = lax.dot_general(jnp.concatenate([h_hi, h_lo], axis=0), rows, (((1,), (1,)), ((), ())),
                            preferred_element_type=F32)
        d = d[0:8] + d[8:16]
        r_scr[pl.ds(t, 1), :] = jnp.sum(jnp.where(own, d, 0.0), axis=0, keepdims=True)

    _token_loop(tm, g_scr, token)
    fold = (lax.broadcasted_iota(jnp.int32, (PEER_SLOTS * 8, PEER_SLOTS), 0) // 8
            == lax.broadcasted_iota(jnp.int32, (PEER_SLOTS * 8, PEER_SLOTS), 1)).astype(BF16)
    r_hi, r_lo = _split_bf16(r_scr[...])
    a = jnp.dot(r_hi, fold, preferred_element_type=F32) + jnp.dot(r_lo, fold, preferred_element_type=F32)
    act = 0.5 * a * (1.0 + lax.erf(a * (2.0 ** -0.5)))
    w_ref[...] = gate_ref[...] * act


def _peer_u(eid, table, hn3, gate, tm):
    t = hn3.shape[0]
    return pl.pallas_call(
        _peer_u_kernel,
        grid=(t // tm,),
        in_specs=[pl.BlockSpec((tm, PEER_SLOTS), lambda i: (i, 0), memory_space=pltpu.SMEM),
                  pl.BlockSpec(memory_space=pltpu.VMEM),
                  pl.BlockSpec((tm, 8, LANES), lambda i: (i, 0, 0)),
                  pl.BlockSpec((tm, PEER_SLOTS), lambda i: (i, 0))],
        out_specs=pl.BlockSpec((tm, PEER_SLOTS), lambda i: (i, 0)),
        out_shape=jax.ShapeDtypeStruct((t, PEER_SLOTS), F32),
        scratch_shapes=[pltpu.VMEM((GATHER_UNROLL, PEER_SLOTS * ROW_WORDS, LANES), jnp.uint32),
                        pltpu.VMEM((tm, PEER_SLOTS * 8), F32)],
        compiler_params=_cparams(1),
        name="peer_u",
    )(eid, table, hn3, gate)


def _peer_v_kernel(eid_ref, w_ref, tab_ref, x_ref, out_ref, g_scr, whi_scr, wlo_scr):
    tm = x_ref.shape[0]
    own = _slot_masks()
    spread = (lax.broadcasted_iota(jnp.int32, (PEER_SLOTS, PEER_SLOTS * 8), 1) // 8
              == lax.broadcasted_iota(jnp.int32, (PEER_SLOTS, PEER_SLOTS * 8), 0)).astype(BF16)
    w_hi, w_lo = _split_bf16(w_ref[...])
    whi_scr[...] = jnp.dot(w_hi, spread, preferred_element_type=F32)
    wlo_scr[...] = jnp.dot(w_lo, spread, preferred_element_type=F32)

    def token(t, g_tile):
        rows = _gather_rows(eid_ref, tab_ref, g_tile, t)
        lhs = jnp.concatenate([jnp.where(own, whi_scr[pl.ds(t, 1), :], 0.0),
                               jnp.where(own, wlo_scr[pl.ds(t, 1), :], 0.0)], axis=0).astype(BF16)
        o = jnp.dot(lhs, rows, preferred_element_type=F32)
        out_ref[t] = x_ref[t] + o[0:8] + o[8:16]

    _token_loop(tm, g_scr, token)


def _peer_v(eid, w, table, x3, tm):
    t = x3.shape[0]
    return pl.pallas_call(
        _peer_v_kernel,
        grid=(t // tm,),
        in_specs=[pl.BlockSpec((tm, PEER_SLOTS), lambda i: (i, 0), memory_space=pltpu.SMEM),
                  pl.BlockSpec((tm, PEER_SLOTS), lambda i: (i, 0)),
                  pl.BlockSpec(memory_space=pltpu.VMEM),
                  pl.BlockSpec((tm, 8, LANES), lambda i: (i, 0, 0))],
        out_specs=pl.BlockSpec((tm, 8, LANES), lambda i: (i, 0, 0)),
        out_shape=jax.ShapeDtypeStruct(x3.shape, F32),
        scratch_shapes=[pltpu.VMEM((GATHER_UNROLL, PEER_SLOTS * ROW_WORDS, LANES), jnp.uint32),
                        pltpu.VMEM((tm, PEER_SLOTS * 8), F32),
                        pltpu.VMEM((tm, PEER_SLOTS * 8), F32)],
        compiler_params=_cparams(1),
        name="peer_v",
    )(eid, w, table, x3)


def _final_norm_kernel(x_ref, g_ref, o_ref):
    o_ref[...] = _rms(x_ref[...], g_ref[...])


def _final_norm(x, g, tm):
    t, d = x.shape
    return pl.pallas_call(
        _final_norm_kernel,
        grid=(t // tm,),
        in_specs=[pl.BlockSpec((tm, d), lambda i: (i, 0)), pl.BlockSpec((1, d), lambda i: (0, 0))],
        out_specs=pl.BlockSpec((tm, d), lambda i: (i, 0)),
        out_shape=jax.ShapeDtypeStruct((t, d), F32),
        compiler_params=_cparams(1),
        name="final_norm",
    )(x, g)


def _rope_angles(pos, dim):
    inv = ROPE_THETA ** (-jnp.arange(0, dim, 2, dtype=F32) / dim)
    ang = pos.astype(F32)[:, None] * inv[None, :]
    return jnp.cos(ang), jnp.sin(ang)


def _rope_table(parts, s_len):
    cs, sn, sp = [], [], []
    for width, cos, sin, half in parts:
        if cos is None:
            cs.append(jnp.ones((s_len, width), F32))
            sn.append(jnp.zeros((s_len, width), F32))
            sp.append(jnp.zeros((s_len, width), F32))
            continue
        reps = width // (2 * half)
        zero = jnp.zeros_like(sin)
        cs.append(jnp.tile(jnp.concatenate([cos, cos], axis=-1), (1, reps)))
        sn.append(jnp.tile(jnp.concatenate([-sin, zero], axis=-1), (1, reps)))
        sp.append(jnp.tile(jnp.concatenate([zero, sin], axis=-1), (1, reps)))
    return jnp.stack([jnp.concatenate(cs, -1), jnp.concatenate(sn, -1), jnp.concatenate(sp, -1)])


def _deinterleave(a, stride):
    if stride == 1:
        return a
    b, h, s, d = a.shape
    return a.reshape(b, h, s // stride, stride, d).transpose(0, 3, 1, 2, 4).reshape(b * stride, h, s // stride, d)


def _reinterleave(a, stride, b):
    if stride == 1:
        return a
    _, l, w = a.shape
    return a.reshape(b, stride, l, w).transpose(0, 2, 1, 3).reshape(b, l * stride, w)


def kernel(x, g_mix, w_in, win_sink, ax_q_norm, ax_k_norm, mla_q_norm, mla_kv_norm, mla_w_uq, mla_w_ukv,
           g_group, w_out, g_ffn, peer_w_q, peer_sub_keys, peer_u, peer_v, g_final):
    b, s, d = x.shape
    t = b * s
    depth = w_in.shape[0]
    hd = HEAD_DIM

    pos = jnp.arange(s)
    cos1, sin1 = _rope_angles(pos, hd)
    cos_m, sin_m = _rope_angles(pos, MLA_ROPE)
    cos_r, sin_r = _rope_angles(pos // GRID_W, hd // 2)
    cos_c, sin_c = _rope_angles(pos % GRID_W, hd // 2)
    taba = _rope_table([(LANES, cos1, sin1, hd // 2)], s)
    tabb = _rope_table([(32, cos_r, sin_r, 16), (32, cos_c, sin_c, 16)] * 2, s)
    tabd = _rope_table([(MLA_NOPE, None, None, 0), (MLA_ROPE, cos_m, sin_m, 16), (32, None, None, 0)], s)
    head_mean = jnp.asarray(np.kron(np.eye(LANES // hd), np.full((hd, hd), 1.0 / hd)), F32)

    tm_prep = min(256, s)
    tm_tok = 256
    tm_gather = 128
    xf = x
    for l in range(depth):
        w_l = w_in[l]
        kr_cols = jnp.concatenate([jnp.zeros((d, MLA_NOPE), F32), w_l[:, 1920:1952], jnp.zeros((d, 32), F32)], -1)
        w_pad = jnp.concatenate([w_l[:, :1920], kr_cols], axis=-1).astype(BF16)
        wuq = mla_w_uq[l].reshape(-1, 4, MLA_NOPE + MLA_ROPE)
        wuq = jnp.pad(wuq, ((0, 0), (0, 0), (0, MLA_PAD - MLA_NOPE - MLA_ROPE))).reshape(-1, 4 * MLA_PAD).astype(BF16)
        wukv = mla_w_ukv[l].reshape(-1, 4, MLA_NOPE + hd)
        wuk = jnp.pad(wukv[:, :, :MLA_NOPE], ((0, 0), (0, 0), (0, MLA_PAD - MLA_NOPE))).reshape(-1, 4 * MLA_PAD).astype(BF16)
        wuv = wukv[:, :, MLA_NOPE:].reshape(-1, 4 * hd).astype(BF16)
        gq = jnp.tile(ax_q_norm[l], 2)[None]
        gk = jnp.tile(ax_k_norm[l], 2)[None]

        (qa, ka, va, qb, kb, vb, qc, kc, vc, qd, kd, vd) = _prep(
            xf.reshape(b, s, d), g_mix[l][None], w_pad, taba, tabb, tabd, gq, gk,
            mla_q_norm[l][None], mla_kv_norm[l][None], wuq, wuk, wuv, head_mean, tm_prep)

        oa = _banded(qa, ka, va, WIN_RADIUS, sink=win_sink[l])
        ob = _dense(qb, kb, vb, tm_prep)
        od = _dense(qd, kd, vd, tm_prep)
        dil = []
        for stride in DIL_STRIDES:
            o, lse = _banded(_deinterleave(qc, stride), _deinterleave(kc, stride), _deinterleave(vc, stride),
                             DIL_RADIUS)
            dil += [_reinterleave(o, stride, b).reshape(t, 4 * hd), _reinterleave(lse, stride, b).reshape(t, 4 * hd)]

        xf = _mix_out(xf.reshape(t, d), oa.reshape(t, 4 * hd), ob.reshape(t, 4 * hd), od.reshape(t, 4 * hd),
                      *dil, g_group[l], w_out[l].astype(BF16), tm_tok)

        hn, eid, gate = _peer_sel(xf, g_ffn[l][None], peer_w_q[l].astype(BF16),
                                  peer_sub_keys[l].astype(BF16), tm_tok)
        wgt = _peer_u(eid, _pack_table(peer_u[l]), hn.reshape(t, 8, LANES), gate, tm_gather)
        xf = _peer_v(eid, wgt, _pack_table(peer_v[l]), xf.reshape(t, 8, LANES), tm_gather).reshape(t, d)

    return _final_norm(xf, g_final[None], tm_tok).reshape(b, s, d)
```

```python
import functools
import math

import jax
import jax.numpy as jnp
import numpy as np
from jax import lax
from jax.experimental import pallas as pl
from jax.experimental.pallas import tpu as pltpu

F32 = jnp.float32
BF16 = jnp.bfloat16

EPS = 1e-6
NEG = -1e30
ROPE_THETA = 10000.0
HEAD_DIM = 64
GRID_W = 64
WIN_RADIUS = 128
DIL_RADIUS = 64
DIL_STRIDES = (1, 4, 16)
MLA_NOPE = 64
MLA_ROPE = 32
MLA_PAD = 128
PEER_KEYS = 128
PEER_HEADS = 8
PEER_TOPK = 16
PEER_SLOTS = PEER_HEADS * PEER_TOPK
IN_PAD = 2048

LANES = 128
ROW_WORDS = 4
GATHER_TILES = 8
VMEM_LIMIT = 56 * 1024 * 1024


def _cparams(n_axes, vmem=None):
    return pltpu.CompilerParams(
        dimension_semantics=("arbitrary",) * n_axes,
        vmem_limit_bytes=vmem or VMEM_LIMIT)


def _rms(x, g):
    return x * lax.rsqrt(jnp.mean(x * x, axis=-1, keepdims=True) + EPS) * g


def _rope(x, tab_ref, shift):
    n = x.shape[-1]
    return (x * tab_ref[0] + pltpu.roll(x, n - shift, 1) * tab_ref[1]
            + pltpu.roll(x, shift, 1) * tab_ref[2])


def _prep_kernel(x_ref, g_ref, w_ref, taba_ref, tabb_ref, tabd_ref, gq_ref, gk_ref,
                 gcq_ref, gckv_ref, wuq_ref, wuk_ref, wuv_ref, hm_ref,
                 qa, ka, va, qb, kb, vb, qc, kc, vc, qd, kd, vd, z_scr):
    h = _rms(x_ref[...], g_ref[...])
    z_scr[...] = jnp.dot(h.astype(BF16), w_ref[...], preferred_element_type=F32)
    hd = HEAD_DIM
    qscale = hd ** -0.5

    def put_heads(dst, first, val):
        for i in range(val.shape[-1] // hd):
            dst[first + i] = val[:, i * hd:(i + 1) * hd].astype(dst.dtype)

    def rotary_group(base, q_dst, k_dst, v_dst):
        for half in range(2):
            q = _rope(z_scr[:, base + half * LANES: base + (half + 1) * LANES], taba_ref, hd // 2)
            put_heads(q_dst, 2 * half, q * qscale)
        put_heads(k_dst, 0, _rope(z_scr[:, base + 256: base + 384], taba_ref, hd // 2))
        put_heads(v_dst, 0, z_scr[:, base + 384: base + 512])

    rotary_group(0, qa, ka, va)
    rotary_group(1024, qc, kc, vc)

    def head_norm(x, g):
        ms = jnp.dot(x * x, hm_ref[...], preferred_element_type=F32)
        return x * lax.rsqrt(ms + EPS) * g

    for half in range(2):
        q = head_norm(z_scr[:, 512 + half * LANES: 512 + (half + 1) * LANES], gq_ref[...])
        put_heads(qb, 2 * half, _rope(q, tabb_ref, hd // 4) * qscale)
    k = head_norm(z_scr[:, 768:896], gk_ref[...])
    put_heads(kb, 0, _rope(k, tabb_ref, hd // 4))
    put_heads(vb, 0, z_scr[:, 896:1024])

    cq = _rms(z_scr[:, 1536:1792], gcq_ref[...]).astype(BF16)
    qfull = jnp.dot(cq, wuq_ref[...], preferred_element_type=F32)
    dscale = (MLA_NOPE + MLA_ROPE) ** -0.5
    ckv = _rms(z_scr[:, 1792:1920], gckv_ref[...]).astype(BF16)
    kfull = jnp.dot(ckv, wuk_ref[...], preferred_element_type=F32)
    vfull = jnp.dot(ckv, wuv_ref[...], preferred_element_type=F32)
    kr = _rope(z_scr[:, 1920:2048], tabd_ref, MLA_ROPE // 2)
    for i in range(4):
        sl = slice(i * MLA_PAD, (i + 1) * MLA_PAD)
        qd[i] = (_rope(qfull[:, sl], tabd_ref, MLA_ROPE // 2) * dscale).astype(qd.dtype)
        kd[i] = (kfull[:, sl] + kr).astype(kd.dtype)
    put_heads(vd, 0, vfull)


def _prep(x, g, w_pad, taba, tabb, tabd, gq, gk, gcq, gckv, wuq, wuk, wuv, hm, tm):
    b, s, d = x.shape
    nt = s // tm
    full = lambda shape: pl.BlockSpec(shape, lambda bi, i: (0,) * len(shape))
    tab = pl.BlockSpec((3, tm, LANES), lambda bi, i: (0, i, 0))

    def hspec(nh, w):
        return pl.BlockSpec((None, nh, tm, w), lambda bi, i: (bi, 0, i, 0))

    def hshape(nh, w):
        return jax.ShapeDtypeStruct((b, nh, s, w), BF16)

    trio_specs = [hspec(4, 64), hspec(2, 64), hspec(2, 64)]
    trio_shapes = [hshape(4, 64), hshape(2, 64), hshape(2, 64)]
    return pl.pallas_call(
        _prep_kernel,
        grid=(b, nt),
        in_specs=[pl.BlockSpec((None, tm, d), lambda bi, i: (bi, i, 0)),
                  full((1, d)), full((d, IN_PAD)), tab, tab, tab,
                  full((1, LANES)), full((1, LANES)), full((1, 256)), full((1, LANES)),
                  full((256, 4 * MLA_PAD)), full((LANES, 4 * MLA_PAD)), full((LANES, 256)),
                  full((LANES, LANES))],
        out_specs=trio_specs * 3 + [hspec(4, MLA_PAD), hspec(4, MLA_PAD), hspec(4, 64)],
        out_shape=trio_shapes * 3 + [hshape(4, MLA_PAD), hshape(4, MLA_PAD), hshape(4, 64)],
        scratch_shapes=[pltpu.VMEM((tm, IN_PAD), F32)],
        compiler_params=_cparams(2),
        name="prep",
    )(x, g, w_pad, taba, tabb, tabd, gq, gk, gcq, gckv, wuq, wuk, wuv, hm)


def _banded_kernel(*refs, radius, tq, win, seq, group, with_sink):
    if with_sink:
        sink_ref, q_ref, k_ref, v_ref, o_ref = refs
    else:
        q_ref, k_ref, v_ref, o_ref, lse_ref = refs
    q0 = pl.program_id(1) * tq
    start = pl.multiple_of(jnp.clip(q0 - radius, 0, seq - win), 64)
    qpos = q0 + lax.broadcasted_iota(jnp.int32, (tq, win), 0)
    kpos = start + lax.broadcasted_iota(jnp.int32, (tq, win), 1)
    valid = jnp.abs(kpos - qpos) <= radius
    hd = HEAD_DIM
    for h in range(q_ref.shape[0]):
        k = k_ref[h // group, pl.ds(start, win), :]
        v = v_ref[h // group, pl.ds(start, win), :]
        s = lax.dot_general(q_ref[h], k, (((1,), (1,)), ((), ())), preferred_element_type=F32)
        s = jnp.where(valid, s, NEG)
        m = jnp.max(s, axis=-1, keepdims=True)
        if with_sink:
            sink = sink_ref[h]
            m = jnp.maximum(m, sink)
        e = jnp.exp(s - m)
        den = jnp.sum(e, axis=-1, keepdims=True)
        if with_sink:
            den = den + jnp.exp(sink - m)
        o = jnp.dot(e.astype(BF16), v, preferred_element_type=F32) / den
        o_ref[:, h * hd:(h + 1) * hd] = o
        if not with_sink:
            lse_ref[:, h * hd:(h + 1) * hd] = jnp.broadcast_to(m + jnp.log(den), (tq, hd))


def _banded(q, k, v, radius, sink=None):
    nb, nh, seq, hd = q.shape
    nkv = k.shape[1]
    tq = min(256, seq)
    win = min(seq, tq + 2 * radius)
    with_sink = sink is not None
    kern = functools.partial(_banded_kernel, radius=radius, tq=tq, win=win, seq=seq,
                             group=nh // nkv, with_sink=with_sink)
    qspec = pl.BlockSpec((None, nh, tq, hd), lambda b, i: (b, 0, i, 0))
    kvspec = pl.BlockSpec((None, nkv, seq, hd), lambda b, i: (b, 0, 0, 0))
    ospec = pl.BlockSpec((None, tq, nh * hd), lambda b, i: (b, i, 0))
    oshape = jax.ShapeDtypeStruct((nb, seq, nh * hd), F32)
    in_specs = [qspec, kvspec, kvspec]
    args = [q, k, v]
    if with_sink:
        in_specs = [pl.BlockSpec(memory_space=pltpu.SMEM)] + in_specs
        args = [sink] + args
    return pl.pallas_call(
        kern,
        grid=(nb, seq // tq),
        in_specs=in_specs,
        out_specs=ospec if with_sink else [ospec, ospec],
        out_shape=oshape if with_sink else [oshape, oshape],
        compiler_params=_cparams(2),
        name="banded_sink" if with_sink else "banded_stats",
    )(*args)


def _dense_kernel(q_ref, k_ref, v_ref, o_ref, *, group):
    hd = v_ref.shape[-1]
    for h in range(q_ref.shape[0]):
        s = lax.dot_general(q_ref[h], k_ref[h // group], (((1,), (1,)), ((), ())),
                            preferred_element_type=F32)
        m = jnp.max(s, axis=-1, keepdims=True)
        e = jnp.exp(s - m)
        den = jnp.sum(e, axis=-1, keepdims=True)
        o = jnp.dot(e.astype(BF16), v_ref[h // group], preferred_element_type=F32)
        o_ref[:, h * hd:(h + 1) * hd] = o / den


def _dense(q, k, v, tq):
    b, nh, s, dk = q.shape
    nkv = k.shape[1]
    hd = v.shape[-1]
    return pl.pallas_call(
        functools.partial(_dense_kernel, group=nh // nkv),
        grid=(b, s // tq),
        in_specs=[pl.BlockSpec((None, nh, tq, dk), lambda bi, i: (bi, 0, i, 0)),
                  pl.BlockSpec((None, nkv, s, dk), lambda bi, i: (bi, 0, 0, 0)),
                  pl.BlockSpec((None, nkv, s, hd), lambda bi, i: (bi, 0, 0, 0))],
        out_specs=pl.BlockSpec((None, tq, nh * hd), lambda bi, i: (bi, i, 0)),
        out_shape=jax.ShapeDtypeStruct((b, s, nh * hd), F32),
        compiler_params=_cparams(2),
        name="dense_attn",
    )(q, k, v)


def _mix_out_kernel(x_ref, oa_ref, ob_ref, od_ref, o1_ref, l1_ref, o4_ref, l4_ref, o16_ref, l16_ref,
                    gg_ref, w_ref, out_ref):
    l1, l4, l16 = l1_ref[...], l4_ref[...], l16_ref[...]
    lm = jnp.maximum(jnp.maximum(l1, l4), l16)
    w1, w4, w16 = jnp.exp(l1 - lm), jnp.exp(l4 - lm), jnp.exp(l16 - lm)
    oc = (w1 * o1_ref[...] + w4 * o4_ref[...] + w16 * o16_ref[...]) / (w1 + w4 + w16)
    gg = gg_ref[...]
    y = jnp.concatenate([_rms(oa_ref[...], gg[0:1]), _rms(ob_ref[...], gg[1:2]),
                         _rms(oc, gg[2:3]), _rms(od_ref[...], gg[3:4])], axis=-1)
    out_ref[...] = x_ref[...] + jnp.dot(y.astype(BF16), w_ref[...], preferred_element_type=F32)


def _mix_out(x, oa, ob, od, o1, l1, o4, l4, o16, l16, gg, w, tm):
    t, d = x.shape
    row = lambda w_: pl.BlockSpec((tm, w_), lambda i: (i, 0))
    full = lambda shape: pl.BlockSpec(shape, lambda i: (0,) * len(shape))
    return pl.pallas_call(
        _mix_out_kernel,
        grid=(t // tm,),
        in_specs=[row(d)] + [row(256)] * 9 + [full(gg.shape), full(w.shape)],
        out_specs=row(d),
        out_shape=jax.ShapeDtypeStruct((t, d), F32),
        compiler_params=_cparams(1),
        name="mix_out",
    )(x, oa, ob, od, o1, l1, o4, l4, o16, l16, gg, w)


def _topk_rows(vals, ids, k):
    best, picked = [], []
    for _ in range(k):
        m = jnp.max(vals, axis=0, keepdims=True)
        idx = jnp.min(jnp.where(vals == m, ids, jnp.float32(2 ** 30)), axis=0, keepdims=True)
        best.append(m)
        picked.append(idx)
        vals = jnp.where(ids == idx, -jnp.inf, vals)
    return jnp.concatenate(best, axis=0), jnp.concatenate(picked, axis=0)


_PAIR_LIMIT = tuple(PEER_TOPK // (i + 1) for i in range(PEER_TOPK))


def _select_experts(s1, i1, s2, i2):
    cols = s1.shape[1]
    row8 = lax.broadcasted_iota(jnp.int32, (8, cols), 0)
    row16 = lax.broadcasted_iota(jnp.int32, (PEER_TOPK, cols), 0).astype(F32)
    cand, flat = [s1[0:1] + s2], [row16]
    for i in range(1, 8):
        cand.append(jnp.where(row8 < _PAIR_LIMIT[i], s1[i:i + 1] + s2[0:8], -jnp.inf))
        flat.append((row8 + i * PEER_TOPK).astype(F32))
    cand.append(s1[8:16] + s2[0:1])
    flat.append(((row8 + 8) * PEER_TOPK).astype(F32))
    best, pick = _topk_rows(jnp.concatenate(cand, axis=0), jnp.concatenate(flat, axis=0), PEER_TOPK)
    pick = pick.astype(jnp.int32)
    pi, pj = pick // PEER_TOPK, pick % PEER_TOPK
    k1 = jnp.zeros(pick.shape, F32)
    k2 = jnp.zeros(pick.shape, F32)
    for r in range(PEER_TOPK):
        k1 = jnp.where(pi == r, i1[r:r + 1], k1)
        k2 = jnp.where(pj == r, i2[r:r + 1], k2)
    e = jnp.exp(best - best[0:1])
    gate = e / jnp.sum(e, axis=0, keepdims=True)
    return gate, ((k1 * PEER_KEYS + k2) * ROW_WORDS).astype(jnp.int32)


def _peer_sel_kernel(x_ref, g_ref, wq_ref, keys_ref, hn_ref, eid_ref, gate_ref, q_scr):
    hn = _rms(x_ref[...], g_ref[...])
    hn_ref[...] = hn
    q_scr[...] = jnp.dot(hn.astype(BF16), wq_ref[...], preferred_element_type=F32).astype(BF16)
    tm = x_ref.shape[0]
    key_ids = lax.broadcasted_iota(jnp.int32, (PEER_KEYS, LANES), 0).astype(F32)
    eids, gates = [], []
    for h in range(PEER_HEADS):
        sc = []
        for c in range(2):
            col = (2 * h + c) * LANES
            sc.append(lax.dot_general(keys_ref[h, c], q_scr[:, col:col + LANES],
                                      (((1,), (1,)), ((), ())), preferred_element_type=F32))
        eid_h, gate_h = [], []
        for part in range(tm // LANES):
            lanes = slice(part * LANES, (part + 1) * LANES)
            s1, i1 = _topk_rows(sc[0][:, lanes], key_ids, PEER_TOPK)
            s2, i2 = _topk_rows(sc[1][:, lanes], key_ids, PEER_TOPK)
            gate, eid = _select_experts(s1, i1, s2, i2)
            eid_h.append(eid)
            gate_h.append(gate)
        eids.append(jnp.concatenate(eid_h, axis=1))
        gates.append(jnp.concatenate(gate_h, axis=1))
    eid_ref[...] = jnp.concatenate(eids, axis=0).T
    gate_ref[...] = jnp.concatenate(gates, axis=0).T


def _peer_sel(x, g, wq, keys, tm):
    t, d = x.shape
    full = lambda shape: pl.BlockSpec(shape, lambda i: (0,) * len(shape))
    return pl.pallas_call(
        _peer_sel_kernel,
        grid=(t // tm,),
        in_specs=[pl.BlockSpec((tm, d), lambda i: (i, 0)), full((1, d)), full(wq.shape), full(keys.shape)],
        out_specs=[pl.BlockSpec((tm, d), lambda i: (i, 0)),
                   pl.BlockSpec((tm, PEER_SLOTS), lambda i: (i, 0)),
                   pl.BlockSpec((tm, PEER_SLOTS), lambda i: (i, 0))],
        out_shape=[jax.ShapeDtypeStruct((t, d), F32),
                   jax.ShapeDtypeStruct((t, PEER_SLOTS), jnp.int32),
                   jax.ShapeDtypeStruct((t, PEER_SLOTS), F32)],
        scratch_shapes=[pltpu.VMEM((tm, wq.shape[1]), BF16)],
        compiler_params=_cparams(1),
        name="peer_sel",
    )(x, g, wq, keys)


def _pack_table(tab):
    n, d = tab.shape
    bits = lax.bitcast_convert_type(tab.astype(BF16), jnp.uint16).astype(jnp.uint32)
    words = bits[:, :d // 2] | (bits[:, d // 2:] << 16)
    return words.reshape(n * ROW_WORDS, LANES)


def _slot_masks():
    col = lax.broadcasted_iota(jnp.int32, (8, PEER_SLOTS * 8), 1) % 8
    row = lax.broadcasted_iota(jnp.int32, (8, PEER_SLOTS * 8), 0)
    return col == 2 * (row % ROW_WORDS) + row // ROW_WORDS


def _split_bf16(x):
    hi = x.astype(BF16)
    return hi, (x - hi.astype(F32)).astype(BF16)


def _gather_rows(eid_ref, tab_ref, g_scr, t):
    for j in range(PEER_SLOTS):
        first_row = pl.multiple_of(eid_ref[t, j], ROW_WORDS)
        g_scr[j * ROW_WORDS:(j + 1) * ROW_WORDS, :] = tab_ref[pl.ds(first_row, ROW_WORDS), :]
    return pltpu.bitcast(g_scr[...], BF16)


def _token_loop(tm, g_scr, token):
    for t in range(tm):
        token(t, g_scr.at[t % GATHER_TILES])


def _peer_u_kernel(eid_ref, tab_ref, hn_ref, gate_ref, w_ref, g_scr, r_scr):
    tm = hn_ref.shape[0]
    own = _slot_masks()

    def token(t, g_tile):
        rows = _gather_rows(eid_ref, tab_ref, g_tile, t)
        h_hi, h_lo = _split_bf16(hn_ref[t])
        d = lax.dot_general(jnp.concatenate([h_hi, h_lo], axis=0), rows, (((1,), (1,)), ((), ())),
                            preferred_element_type=F32)
        d = d[0:8] + d[8:16]
        r_scr[pl.ds(t, 1), :] = jnp.sum(jnp.where(own, d, 0.0), axis=0, keepdims=True)

    _token_loop(tm, g_scr, token)
    fold = (lax.broadcasted_iota(jnp.int32, (PEER_SLOTS * 8, PEER_SLOTS), 0) // 8
            == lax.broadcasted_iota(jnp.int32, (PEER_SLOTS * 8, PEER_SLOTS), 1)).astype(BF16)
    r_hi, r_lo = _split_bf16(r_scr[...])
    a = jnp.dot(r_hi, fold, preferred_element_type=F32) + jnp.dot(r_lo, fold, preferred_element_type=F32)
    act = 0.5 * a * (1.0 + lax.erf(a * (2.0 ** -0.5)))
    w_ref[...] = gate_ref[...] * act


def _peer_u(eid, table, hn3, gate, tm):
    t = hn3.shape[0]
    return pl.pallas_call(
        _peer_u_kernel,
        grid=(t // tm,),
        in_specs=[pl.BlockSpec((tm, PEER_SLOTS), lambda i: (i, 0), memory_space=pltpu.SMEM,
                               pipeline_mode=pl.Buffered(1)),
                  pl.BlockSpec(memory_space=pltpu.VMEM),
                  pl.BlockSpec((tm, 8, LANES), lambda i: (i, 0, 0)),
                  pl.BlockSpec((tm, PEER_SLOTS), lambda i: (i, 0))],
        out_specs=pl.BlockSpec((tm, PEER_SLOTS), lambda i: (i, 0)),
        out_shape=jax.ShapeDtypeStruct((t, PEER_SLOTS), F32),
        scratch_shapes=[pltpu.VMEM((GATHER_TILES, PEER_SLOTS * ROW_WORDS, LANES), jnp.uint32),
                        pltpu.VMEM((tm, PEER_SLOTS * 8), F32)],
        compiler_params=_cparams(1),
        name="peer_u",
    )(eid, table, hn3, gate)


def _peer_v_kernel(eid_ref, w_ref, tab_ref, x_ref, out_ref, g_scr, whi_scr, wlo_scr):
    tm = x_ref.shape[0]
    own = _slot_masks()
    spread = (lax.broadcasted_iota(jnp.int32, (PEER_SLOTS, PEER_SLOTS * 8), 1) // 8
              == lax.broadcasted_iota(jnp.int32, (PEER_SLOTS, PEER_SLOTS * 8), 0)).astype(BF16)
    w_hi, w_lo = _split_bf16(w_ref[...])
    whi_scr[...] = jnp.dot(w_hi, spread, preferred_element_type=F32)
    wlo_scr[...] = jnp.dot(w_lo, spread, preferred_element_type=F32)

    def token(t, g_tile):
        rows = _gather_rows(eid_ref, tab_ref, g_tile, t)
        lhs = jnp.concatenate([jnp.where(own, whi_scr[pl.ds(t, 1), :], 0.0),
                               jnp.where(own, wlo_scr[pl.ds(t, 1), :], 0.0)], axis=0).astype(BF16)
        o = jnp.dot(lhs, rows, preferred_element_type=F32)
        out_ref[t] = x_ref[t] + o[0:8] + o[8:16]

    _token_loop(tm, g_scr, token)


def _peer_v(eid, w, table, x3, tm):
    t = x3.shape[0]
    return pl.pallas_call(
        _peer_v_kernel,
        grid=(t // tm,),
        in_specs=[pl.BlockSpec((tm, PEER_SLOTS), lambda i: (i, 0), memory_space=pltpu.SMEM,
                               pipeline_mode=pl.Buffered(1)),
                  pl.BlockSpec((tm, PEER_SLOTS), lambda i: (i, 0)),
                  pl.BlockSpec(memory_space=pltpu.VMEM),
                  pl.BlockSpec((tm, 8, LANES), lambda i: (i, 0, 0))],
        out_specs=pl.BlockSpec((tm, 8, LANES), lambda i: (i, 0, 0)),
        out_shape=jax.ShapeDtypeStruct(x3.shape, F32),
        scratch_shapes=[pltpu.VMEM((GATHER_TILES, PEER_SLOTS * ROW_WORDS, LANES), jnp.uint32),
                        pltpu.VMEM((tm, PEER_SLOTS * 8), F32),
                        pltpu.VMEM((tm, PEER_SLOTS * 8), F32)],
        compiler_params=_cparams(1),
        name="peer_v",
    )(eid, w, table, x3)


def _final_norm_kernel(x_ref, g_ref, o_ref):
    o_ref[...] = _rms(x_ref[...], g_ref[...])


def _final_norm(x, g, tm):
    t, d = x.shape
    return pl.pallas_call(
        _final_norm_kernel,
        grid=(t // tm,),
        in_specs=[pl.BlockSpec((tm, d), lambda i: (i, 0)), pl.BlockSpec((1, d), lambda i: (0, 0))],
        out_specs=pl.BlockSpec((tm, d), lambda i: (i, 0)),
        out_shape=jax.ShapeDtypeStruct((t, d), F32),
        compiler_params=_cparams(1),
        name="final_norm",
    )(x, g)


def _rope_angles(pos, dim):
    inv = ROPE_THETA ** (-jnp.arange(0, dim, 2, dtype=F32) / dim)
    ang = pos.astype(F32)[:, None] * inv[None, :]
    return jnp.cos(ang), jnp.sin(ang)


def _rope_table(parts, s_len):
    cs, sn, sp = [], [], []
    for width, cos, sin, half in parts:
        if cos is None:
            cs.append(jnp.ones((s_len, width), F32))
            sn.append(jnp.zeros((s_len, width), F32))
            sp.append(jnp.zeros((s_len, width), F32))
            continue
        reps = width // (2 * half)
        zero = jnp.zeros_like(sin)
        cs.append(jnp.tile(jnp.concatenate([cos, cos], axis=-1), (1, reps)))
        sn.append(jnp.tile(jnp.concatenate([-sin, zero], axis=-1), (1, reps)))
        sp.append(jnp.tile(jnp.concatenate([zero, sin], axis=-1), (1, reps)))
    return jnp.stack([jnp.concatenate(cs, -1), jnp.concatenate(sn, -1), jnp.concatenate(sp, -1)])


def _deinterleave(a, stride):
    if stride == 1:
        return a
    b, h, s, d = a.shape
    return a.reshape(b, h, s // stride, stride, d).transpose(0, 3, 1, 2, 4).reshape(b * stride, h, s // stride, d)


def _reinterleave(a, stride, b):
    if stride == 1:
        return a
    _, l, w = a.shape
    return a.reshape(b, stride, l, w).transpose(0, 2, 1, 3).reshape(b, l * stride, w)


def kernel(x, g_mix, w_in, win_sink, ax_q_norm, ax_k_norm, mla_q_norm, mla_kv_norm, mla_w_uq, mla_w_ukv,
           g_group, w_out, g_ffn, peer_w_q, peer_sub_keys, peer_u, peer_v, g_final):
    b, s, d = x.shape
    t = b * s
    depth = w_in.shape[0]
    hd = HEAD_DIM

    pos = jnp.arange(s)
    cos1, sin1 = _rope_angles(pos, hd)
    cos_m, sin_m = _rope_angles(pos, MLA_ROPE)
    cos_r, sin_r = _rope_angles(pos // GRID_W, hd // 2)
    cos_c, sin_c = _rope_angles(pos % GRID_W, hd // 2)
    taba = _rope_table([(LANES, cos1, sin1, hd // 2)], s)
    tabb = _rope_table([(32, cos_r, sin_r, 16), (32, cos_c, sin_c, 16)] * 2, s)
    tabd = _rope_table([(MLA_NOPE, None, None, 0), (MLA_ROPE, cos_m, sin_m, 16), (32, None, None, 0)], s)
    head_mean = jnp.asarray(np.kron(np.eye(LANES // hd), np.full((hd, hd), 1.0 / hd)), F32)

    tm_prep = min(256, s)
    tm_tok = 256
    tm_gather = 128
    xf = x
    for l in range(depth):
        w_l = w_in[l]
        kr_cols = jnp.concatenate([jnp.zeros((d, MLA_NOPE), F32), w_l[:, 1920:1952], jnp.zeros((d, 32), F32)], -1)
        w_pad = jnp.concatenate([w_l[:, :1920], kr_cols], axis=-1).astype(BF16)
        wuq = mla_w_uq[l].reshape(-1, 4, MLA_NOPE + MLA_ROPE)
        wuq = jnp.pad(wuq, ((0, 0), (0, 0), (0, MLA_PAD - MLA_NOPE - MLA_ROPE))).reshape(-1, 4 * MLA_PAD).astype(BF16)
        wukv = mla_w_ukv[l].reshape(-1, 4, MLA_NOPE + hd)
        wuk = jnp.pad(wukv[:, :, :MLA_NOPE], ((0, 0), (0, 0), (0, MLA_PAD - MLA_NOPE))).reshape(-1, 4 * MLA_PAD).astype(BF16)
        wuv = wukv[:, :, MLA_NOPE:].reshape(-1, 4 * hd).astype(BF16)
        gq = jnp.tile(ax_q_norm[l], 2)[None]
        gk = jnp.tile(ax_k_norm[l], 2)[None]

        (qa, ka, va, qb, kb, vb, qc, kc, vc, qd, kd, vd) = _prep(
            xf.reshape(b, s, d), g_mix[l][None], w_pad, taba, tabb, tabd, gq, gk,
            mla_q_norm[l][None], mla_kv_norm[l][None], wuq, wuk, wuv, head_mean, tm_prep)

        oa = _banded(qa, ka, va, WIN_RADIUS, sink=win_sink[l])
        ob = _dense(qb, kb, vb, tm_prep)
        od = _dense(qd, kd, vd, tm_prep)
        dil = []
        for stride in DIL_STRIDES:
            o, lse = _banded(_deinterleave(qc, stride), _deinterleave(kc, stride), _deinterleave(vc, stride),
                             DIL_RADIUS)
            dil += [_reinterleave(o, stride, b).reshape(t, 4 * hd), _reinterleave(lse, stride, b).reshape(t, 4 * hd)]

        xf = _mix_out(xf.reshape(t, d), oa.reshape(t, 4 * hd), ob.reshape(t, 4 * hd), od.reshape(t, 4 * hd),
                      *dil, g_group[l], w_out[l].astype(BF16), tm_tok)

        hn, eid, gate = _peer_sel(xf, g_ffn[l][None], peer_w_q[l].astype(BF16),
                                  peer_sub_keys[l].astype(BF16), tm_tok)
        wgt = _peer_u(eid, _pack_table(peer_u[l]), hn.reshape(t, 8, LANES), gate, tm_gather)
        xf = _peer_v(eid, wgt, _pack_table(peer_v[l]), xf.reshape(t, 8, LANES), tm_gather).reshape(t, d)

    return _final_norm(xf, g_final[None], tm_tok).reshape(b, s, d)
```

```python
import functools
import math

import jax
import jax.numpy as jnp
import numpy as np
from jax import lax
from jax.experimental import pallas as pl
from jax.experimental.pallas import tpu as pltpu

F32 = jnp.float32
BF16 = jnp.bfloat16

EPS = 1e-6
NEG = -1e30
ROPE_THETA = 10000.0
HEAD_DIM = 64
GRID_W = 64
WIN_RADIUS = 128
DIL_RADIUS = 64
DIL_STRIDES = (1, 4, 16)
MLA_NOPE = 64
MLA_ROPE = 32
MLA_PAD = 128
PEER_KEYS = 128
PEER_HEADS = 8
PEER_TOPK = 16
PEER_SLOTS = PEER_HEADS * PEER_TOPK
IN_PAD = 2048

LANES = 128
ROW_WORDS = 4
VMEM_LIMIT = 56 * 1024 * 1024


def _cparams(n_axes, vmem=None):
    return pltpu.CompilerParams(
        dimension_semantics=("arbitrary",) * n_axes,
        vmem_limit_bytes=vmem or VMEM_LIMIT)


def _rms(x, g):
    return x * lax.rsqrt(jnp.mean(x * x, axis=-1, keepdims=True) + EPS) * g


def _rope(x, tab_ref, shift):
    n = x.shape[-1]
    return (x * tab_ref[0] + pltpu.roll(x, n - shift, 1) * tab_ref[1]
            + pltpu.roll(x, shift, 1) * tab_ref[2])


def _prep_kernel(x_ref, g_ref, w_ref, taba_ref, tabb_ref, tabd_ref, gq_ref, gk_ref,
                 gcq_ref, gckv_ref, wuq_ref, wuk_ref, wuv_ref, hm_ref,
                 qa, ka, va, qb, kb, vb, qc, kc, vc, qd, kd, vd, z_scr):
    h = _rms(x_ref[...], g_ref[...])
    z_scr[...] = jnp.dot(h.astype(BF16), w_ref[...], preferred_element_type=F32)
    hd = HEAD_DIM
    qscale = hd ** -0.5

    def put_heads(dst, first, val):
        for i in range(val.shape[-1] // hd):
            dst[first + i] = val[:, i * hd:(i + 1) * hd].astype(dst.dtype)

    def rotary_group(base, q_dst, k_dst, v_dst):
        for half in range(2):
            q = _rope(z_scr[:, base + half * LANES: base + (half + 1) * LANES], taba_ref, hd // 2)
            put_heads(q_dst, 2 * half, q * qscale)
        put_heads(k_dst, 0, _rope(z_scr[:, base + 256: base + 384], taba_ref, hd // 2))
        put_heads(v_dst, 0, z_scr[:, base + 384: base + 512])

    rotary_group(0, qa, ka, va)
    rotary_group(1024, qc, kc, vc)

    def head_norm(x, g):
        ms = jnp.dot(x * x, hm_ref[...], preferred_element_type=F32)
        return x * lax.rsqrt(ms + EPS) * g

    for half in range(2):
        q = head_norm(z_scr[:, 512 + half * LANES: 512 + (half + 1) * LANES], gq_ref[...])
        put_heads(qb, 2 * half, _rope(q, tabb_ref, hd // 4) * qscale)
    k = head_norm(z_scr[:, 768:896], gk_ref[...])
    put_heads(kb, 0, _rope(k, tabb_ref, hd // 4))
    put_heads(vb, 0, z_scr[:, 896:1024])

    cq = _rms(z_scr[:, 1536:1792], gcq_ref[...]).astype(BF16)
    qfull = jnp.dot(cq, wuq_ref[...], preferred_element_type=F32)
    dscale = (MLA_NOPE + MLA_ROPE) ** -0.5
    ckv = _rms(z_scr[:, 1792:1920], gckv_ref[...]).astype(BF16)
    kfull = jnp.dot(ckv, wuk_ref[...], preferred_element_type=F32)
    vfull = jnp.dot(ckv, wuv_ref[...], preferred_element_type=F32)
    kr = _rope(z_scr[:, 1920:2048], tabd_ref, MLA_ROPE // 2)
    for i in range(4):
        sl = slice(i * MLA_PAD, (i + 1) * MLA_PAD)
        qd[i] = (_rope(qfull[:, sl], tabd_ref, MLA_ROPE // 2) * dscale).astype(qd.dtype)
        kd[i] = (kfull[:, sl] + kr).astype(kd.dtype)
    put_heads(vd, 0, vfull)


def _prep(x, g, w_pad, taba, tabb, tabd, gq, gk, gcq, gckv, wuq, wuk, wuv, hm, tm):
    b, s, d = x.shape
    nt = s // tm
    full = lambda shape: pl.BlockSpec(shape, lambda bi, i: (0,) * len(shape))
    tab = pl.BlockSpec((3, tm, LANES), lambda bi, i: (0, i, 0))

    def hspec(nh, w):
        return pl.BlockSpec((None, nh, tm, w), lambda bi, i: (bi, 0, i, 0))

    def hshape(nh, w):
        return jax.ShapeDtypeStruct((b, nh, s, w), BF16)

    trio_specs = [hspec(4, 64), hspec(2, 64), hspec(2, 64)]
    trio_shapes = [hshape(4, 64), hshape(2, 64), hshape(2, 64)]
    return pl.pallas_call(
        _prep_kernel,
        grid=(b, nt),
        in_specs=[pl.BlockSpec((None, tm, d), lambda bi, i: (bi, i, 0)),
                  full((1, d)), full((d, IN_PAD)), tab, tab, tab,
                  full((1, LANES)), full((1, LANES)), full((1, 256)), full((1, LANES)),
                  full((256, 4 * MLA_PAD)), full((LANES, 4 * MLA_PAD)), full((LANES, 256)),
                  full((LANES, LANES))],
        out_specs=trio_specs * 3 + [hspec(4, MLA_PAD), hspec(4, MLA_PAD), hspec(4, 64)],
        out_shape=trio_shapes * 3 + [hshape(4, MLA_PAD), hshape(4, MLA_PAD), hshape(4, 64)],
        scratch_shapes=[pltpu.VMEM((tm, IN_PAD), F32)],
        compiler_params=_cparams(2),
        name="prep",
    )(x, g, w_pad, taba, tabb, tabd, gq, gk, gcq, gckv, wuq, wuk, wuv, hm)


def _banded_kernel(*refs, radius, tq, win, seq, group, with_sink):
    if with_sink:
        sink_ref, q_ref, k_ref, v_ref, o_ref = refs
    else:
        q_ref, k_ref, v_ref, o_ref, lse_ref = refs
    q0 = pl.program_id(1) * tq
    start = pl.multiple_of(jnp.clip(q0 - radius, 0, seq - win), 64)
    qpos = q0 + lax.broadcasted_iota(jnp.int32, (tq, win), 0)
    kpos = start + lax.broadcasted_iota(jnp.int32, (tq, win), 1)
    valid = jnp.abs(kpos - qpos) <= radius
    hd = HEAD_DIM
    for h in range(q_ref.shape[0]):
        k = k_ref[h // group, pl.ds(start, win), :]
        v = v_ref[h // group, pl.ds(start, win), :]
        s = lax.dot_general(q_ref[h], k, (((1,), (1,)), ((), ())), preferred_element_type=F32)
        s = jnp.where(valid, s, NEG)
        m = jnp.max(s, axis=-1, keepdims=True)
        if with_sink:
            sink = sink_ref[h]
            m = jnp.maximum(m, sink)
        e = jnp.exp(s - m)
        den = jnp.sum(e, axis=-1, keepdims=True)
        if with_sink:
            den = den + jnp.exp(sink - m)
        o = jnp.dot(e.astype(BF16), v, preferred_element_type=F32) / den
        o_ref[:, h * hd:(h + 1) * hd] = o
        if not with_sink:
            lse_ref[:, h * hd:(h + 1) * hd] = jnp.broadcast_to(m + jnp.log(den), (tq, hd))


def _banded(q, k, v, radius, sink=None):
    nb, nh, seq, hd = q.shape
    nkv = k.shape[1]
    tq = min(256, seq)
    win = min(seq, tq + 2 * radius)
    with_sink = sink is not None
    kern = functools.partial(_banded_kernel, radius=radius, tq=tq, win=win, seq=seq,
                             group=nh // nkv, with_sink=with_sink)
    qspec = pl.BlockSpec((None, nh, tq, hd), lambda b, i: (b, 0, i, 0))
    kvspec = pl.BlockSpec((None, nkv, seq, hd), lambda b, i: (b, 0, 0, 0))
    ospec = pl.BlockSpec((None, tq, nh * hd), lambda b, i: (b, i, 0))
    oshape = jax.ShapeDtypeStruct((nb, seq, nh * hd), F32)
    in_specs = [qspec, kvspec, kvspec]
    args = [q, k, v]
    if with_sink:
        in_specs = [pl.BlockSpec(memory_space=pltpu.SMEM)] + in_specs
        args = [sink] + args
    return pl.pallas_call(
        kern,
        grid=(nb, seq // tq),
        in_specs=in_specs,
        out_specs=ospec if with_sink else [ospec, ospec],
        out_shape=oshape if with_sink else [oshape, oshape],
        compiler_params=_cparams(2),
        name="banded_sink" if with_sink else "banded_stats",
    )(*args)


def _dense_kernel(q_ref, k_ref, v_ref, o_ref, *, group):
    hd = v_ref.shape[-1]
    for h in range(q_ref.shape[0]):
        s = lax.dot_general(q_ref[h], k_ref[h // group], (((1,), (1,)), ((), ())),
                            preferred_element_type=F32)
        m = jnp.max(s, axis=-1, keepdims=True)
        e = jnp.exp(s - m)
        den = jnp.sum(e, axis=-1, keepdims=True)
        o = jnp.dot(e.astype(BF16), v_ref[h // group], preferred_element_type=F32)
        o_ref[:, h * hd:(h + 1) * hd] = o / den


def _dense(q, k, v, tq):
    b, nh, s, dk = q.shape
    nkv = k.shape[1]
    hd = v.shape[-1]
    return pl.pallas_call(
        functools.partial(_dense_kernel, group=nh // nkv),
        grid=(b, s // tq),
        in_specs=[pl.BlockSpec((None, nh, tq, dk), lambda bi, i: (bi, 0, i, 0)),
                  pl.BlockSpec((None, nkv, s, dk), lambda bi, i: (bi, 0, 0, 0)),
                  pl.BlockSpec((None, nkv, s, hd), lambda bi, i: (bi, 0, 0, 0))],
        out_specs=pl.BlockSpec((None, tq, nh * hd), lambda bi, i: (bi, i, 0)),
        out_shape=jax.ShapeDtypeStruct((b, s, nh * hd), F32),
        compiler_params=_cparams(2),
        name="dense_attn",
    )(q, k, v)


def _mix_out_kernel(x_ref, oa_ref, ob_ref, od_ref, o1_ref, l1_ref, o4_ref, l4_ref, o16_ref, l16_ref,
                    gg_ref, w_ref, out_ref):
    l1, l4, l16 = l1_ref[...], l4_ref[...], l16_ref[...]
    lm = jnp.maximum(jnp.maximum(l1, l4), l16)
    w1, w4, w16 = jnp.exp(l1 - lm), jnp.exp(l4 - lm), jnp.exp(l16 - lm)
    oc = (w1 * o1_ref[...] + w4 * o4_ref[...] + w16 * o16_ref[...]) / (w1 + w4 + w16)
    gg = gg_ref[...]
    y = jnp.concatenate([_rms(oa_ref[...], gg[0:1]), _rms(ob_ref[...], gg[1:2]),
                         _rms(oc, gg[2:3]), _rms(od_ref[...], gg[3:4])], axis=-1)
    out_ref[...] = x_ref[...] + jnp.dot(y.astype(BF16), w_ref[...], preferred_element_type=F32)


def _mix_out(x, oa, ob, od, o1, l1, o4, l4, o16, l16, gg, w, tm):
    t, d = x.shape
    row = lambda w_: pl.BlockSpec((tm, w_), lambda i: (i, 0))
    full = lambda shape: pl.BlockSpec(shape, lambda i: (0,) * len(shape))
    return pl.pallas_call(
        _mix_out_kernel,
        grid=(t // tm,),
        in_specs=[row(d)] + [row(256)] * 9 + [full(gg.shape), full(w.shape)],
        out_specs=row(d),
        out_shape=jax.ShapeDtypeStruct((t, d), F32),
        compiler_params=_cparams(1),
        name="mix_out",
    )(x, oa, ob, od, o1, l1, o4, l4, o16, l16, gg, w)


def _topk_rows(vals, ids, k):
    best, picked = [], []
    for _ in range(k):
        m = jnp.max(vals, axis=0, keepdims=True)
        idx = jnp.min(jnp.where(vals == m, ids, jnp.float32(2 ** 30)), axis=0, keepdims=True)
        best.append(m)
        picked.append(idx)
        vals = jnp.where(ids == idx, -jnp.inf, vals)
    return jnp.concatenate(best, axis=0), jnp.concatenate(picked, axis=0)


_PAIR_LIMIT = tuple(PEER_TOPK // (i + 1) for i in range(PEER_TOPK))


def _select_experts(s1, i1, s2, i2):
    cols = s1.shape[1]
    row8 = lax.broadcasted_iota(jnp.int32, (8, cols), 0)
    row16 = lax.broadcasted_iota(jnp.int32, (PEER_TOPK, cols), 0).astype(F32)
    cand, flat = [s1[0:1] + s2], [row16]
    for i in range(1, 8):
        cand.append(jnp.where(row8 < _PAIR_LIMIT[i], s1[i:i + 1] + s2[0:8], -jnp.inf))
        flat.append((row8 + i * PEER_TOPK).astype(F32))
    cand.append(s1[8:16] + s2[0:1])
    flat.append(((row8 + 8) * PEER_TOPK).astype(F32))
    best, pick = _topk_rows(jnp.concatenate(cand, axis=0), jnp.concatenate(flat, axis=0), PEER_TOPK)
    pick = pick.astype(jnp.int32)
    pi, pj = pick // PEER_TOPK, pick % PEER_TOPK
    k1 = jnp.zeros(pick.shape, F32)
    k2 = jnp.zeros(pick.shape, F32)
    for r in range(PEER_TOPK):
        k1 = jnp.where(pi == r, i1[r:r + 1], k1)
        k2 = jnp.where(pj == r, i2[r:r + 1], k2)
    e = jnp.exp(best - best[0:1])
    gate = e / jnp.sum(e, axis=0, keepdims=True)
    odd_slot = lax.broadcasted_iota(jnp.int32, pick.shape, 0) % 2
    return gate, (k1 * PEER_KEYS + k2 + 1).astype(jnp.int32) * ROW_WORDS - odd_slot * ROW_WORDS


def _peer_sel_kernel(x_ref, g_ref, wq_ref, keys_ref, hn_ref, eid_ref, gate_ref, q_scr):
    hn = _rms(x_ref[...], g_ref[...])
    hn_ref[...] = hn
    q_scr[...] = jnp.dot(hn.astype(BF16), wq_ref[...], preferred_element_type=F32).astype(BF16)
    tm = x_ref.shape[0]
    key_ids = lax.broadcasted_iota(jnp.int32, (PEER_KEYS, LANES), 0).astype(F32)
    eids, gates = [], []
    for h in range(PEER_HEADS):
        sc = []
        for c in range(2):
            col = (2 * h + c) * LANES
            sc.append(lax.dot_general(keys_ref[h, c], q_scr[:, col:col + LANES],
                                      (((1,), (1,)), ((), ())), preferred_element_type=F32))
        eid_h, gate_h = [], []
        for part in range(tm // LANES):
            lanes = slice(part * LANES, (part + 1) * LANES)
            s1, i1 = _topk_rows(sc[0][:, lanes], key_ids, PEER_TOPK)
            s2, i2 = _topk_rows(sc[1][:, lanes], key_ids, PEER_TOPK)
            gate, eid = _select_experts(s1, i1, s2, i2)
            eid_h.append(eid)
            gate_h.append(gate)
        eids.append(jnp.concatenate(eid_h, axis=1))
        gates.append(jnp.concatenate(gate_h, axis=1))
    eid_ref[...] = jnp.concatenate(eids, axis=0).T
    gate_ref[...] = jnp.concatenate(gates, axis=0).T


def _peer_sel(x, g, wq, keys, tm):
    t, d = x.shape
    full = lambda shape: pl.BlockSpec(shape, lambda i: (0,) * len(shape))
    return pl.pallas_call(
        _peer_sel_kernel,
        grid=(t // tm,),
        in_specs=[pl.BlockSpec((tm, d), lambda i: (i, 0)), full((1, d)), full(wq.shape), full(keys.shape)],
        out_specs=[pl.BlockSpec((tm, d), lambda i: (i, 0)),
                   pl.BlockSpec((tm, PEER_SLOTS), lambda i: (i, 0)),
                   pl.BlockSpec((tm, PEER_SLOTS), lambda i: (i, 0))],
        out_shape=[jax.ShapeDtypeStruct((t, d), F32),
                   jax.ShapeDtypeStruct((t, PEER_SLOTS), jnp.int32),
                   jax.ShapeDtypeStruct((t, PEER_SLOTS), F32)],
        scratch_shapes=[pltpu.VMEM((tm, wq.shape[1]), BF16)],
        compiler_params=_cparams(1),
        name="peer_sel",
    )(x, g, wq, keys)


def _pack_table(tab):
    n, d = tab.shape
    bits = lax.bitcast_convert_type(tab.astype(BF16), jnp.uint16).astype(jnp.uint32)
    words = bits[:, :d // 2] | (bits[:, d // 2:] << 16)
    pad = jnp.zeros((ROW_WORDS, LANES), jnp.uint32)
    return jnp.concatenate([pad, words.reshape(n * ROW_WORDS, LANES), pad], axis=0)


def _slot_masks():
    col = lax.broadcasted_iota(jnp.int32, (8, PEER_SLOTS * 8), 1) % 8
    row = lax.broadcasted_iota(jnp.int32, (8, PEER_SLOTS * 8), 0)
    return col == 2 * (row % ROW_WORDS) + row // ROW_WORDS


def _split_bf16(x):
    hi = x.astype(BF16)
    return hi, (x - hi.astype(F32)).astype(BF16)


def _gather_rows(eid_ref, tab_ref, t):
    low_half = lax.broadcasted_iota(jnp.int32, (8, LANES), 0) < ROW_WORDS
    pairs = []
    for j in range(0, PEER_SLOTS, 2):
        even = pl.multiple_of(eid_ref[t, j], ROW_WORDS)
        odd = pl.multiple_of(eid_ref[t, j + 1], ROW_WORDS)
        pairs.append(jnp.where(low_half, tab_ref[pl.ds(even, 8), :], tab_ref[pl.ds(odd, 8), :]))
    return pltpu.bitcast(jnp.concatenate(pairs, axis=0), BF16)


def _peer_u_kernel(eid_ref, tab_ref, hn_ref, gate_ref, w_ref, r_scr):
    tm = hn_ref.shape[0]
    own = _slot_masks()

    for t in range(tm):
        rows = _gather_rows(eid_ref, tab_ref, t)
        h_hi, h_lo = _split_bf16(hn_ref[t])
        d = lax.dot_general(jnp.concatenate([h_hi, h_lo], axis=0), rows, (((1,), (1,)), ((), ())),
                            preferred_element_type=F32)
        d = d[0:8] + d[8:16]
        r_scr[pl.ds(t, 1), :] = jnp.sum(jnp.where(own, d, 0.0), axis=0, keepdims=True)

    fold = (lax.broadcasted_iota(jnp.int32, (PEER_SLOTS * 8, PEER_SLOTS), 0) // 8
            == lax.broadcasted_iota(jnp.int32, (PEER_SLOTS * 8, PEER_SLOTS), 1)).astype(BF16)
    r_hi, r_lo = _split_bf16(r_scr[...])
    a = jnp.dot(r_hi, fold, preferred_element_type=F32) + jnp.dot(r_lo, fold, preferred_element_type=F32)
    act = 0.5 * a * (1.0 + lax.erf(a * (2.0 ** -0.5)))
    w_ref[...] = gate_ref[...] * act


def _peer_u(eid, table, hn3, gate, tm):
    t = hn3.shape[0]
    return pl.pallas_call(
        _peer_u_kernel,
        grid=(t // tm,),
        in_specs=[pl.BlockSpec((tm, PEER_SLOTS), lambda i: (i, 0), memory_space=pltpu.SMEM,
                               pipeline_mode=pl.Buffered(1)),
                  pl.BlockSpec(memory_space=pltpu.VMEM),
                  pl.BlockSpec((tm, 8, LANES), lambda i: (i, 0, 0)),
                  pl.BlockSpec((tm, PEER_SLOTS), lambda i: (i, 0))],
        out_specs=pl.BlockSpec((tm, PEER_SLOTS), lambda i: (i, 0)),
        out_shape=jax.ShapeDtypeStruct((t, PEER_SLOTS), F32),
        scratch_shapes=[pltpu.VMEM((tm, PEER_SLOTS * 8), F32)],
        compiler_params=_cparams(1),
        name="peer_u",
    )(eid, table, hn3, gate)


def _peer_v_kernel(eid_ref, w_ref, tab_ref, x_ref, out_ref, whi_scr, wlo_scr):
    tm = x_ref.shape[0]
    own = _slot_masks()
    spread = (lax.broadcasted_iota(jnp.int32, (PEER_SLOTS, PEER_SLOTS * 8), 1) // 8
              == lax.broadcasted_iota(jnp.int32, (PEER_SLOTS, PEER_SLOTS * 8), 0)).astype(BF16)
    w_hi, w_lo = _split_bf16(w_ref[...])
    whi_scr[...] = jnp.dot(w_hi, spread, preferred_element_type=F32)
    wlo_scr[...] = jnp.dot(w_lo, spread, preferred_element_type=F32)

    for t in range(tm):
        rows = _gather_rows(eid_ref, tab_ref, t)
        lhs = jnp.concatenate([jnp.where(own, whi_scr[pl.ds(t, 1), :], 0.0),
                               jnp.where(own, wlo_scr[pl.ds(t, 1), :], 0.0)], axis=0).astype(BF16)
        o = jnp.dot(lhs, rows, preferred_element_type=F32)
        out_ref[t] = x_ref[t] + o[0:8] + o[8:16]


def _peer_v(eid, w, table, x3, tm):
    t = x3.shape[0]
    return pl.pallas_call(
        _peer_v_kernel,
        grid=(t // tm,),
        in_specs=[pl.BlockSpec((tm, PEER_SLOTS), lambda i: (i, 0), memory_space=pltpu.SMEM,
                               pipeline_mode=pl.Buffered(1)),
                  pl.BlockSpec((tm, PEER_SLOTS), lambda i: (i, 0)),
                  pl.BlockSpec(memory_space=pltpu.VMEM),
                  pl.BlockSpec((tm, 8, LANES), lambda i: (i, 0, 0))],
        out_specs=pl.BlockSpec((tm, 8, LANES), lambda i: (i, 0, 0)),
        out_shape=jax.ShapeDtypeStruct(x3.shape, F32),
        scratch_shapes=[pltpu.VMEM((tm, PEER_SLOTS * 8), F32),
                        pltpu.VMEM((tm, PEER_SLOTS * 8), F32)],
        compiler_params=_cparams(1),
        name="peer_v",
    )(eid, w, table, x3)


def _final_norm_kernel(x_ref, g_ref, o_ref):
    o_ref[...] = _rms(x_ref[...], g_ref[...])


def _final_norm(x, g, tm):
    t, d = x.shape
    return pl.pallas_call(
        _final_norm_kernel,
        grid=(t // tm,),
        in_specs=[pl.BlockSpec((tm, d), lambda i: (i, 0)), pl.BlockSpec((1, d), lambda i: (0, 0))],
        out_specs=pl.BlockSpec((tm, d), lambda i: (i, 0)),
        out_shape=jax.ShapeDtypeStruct((t, d), F32),
        compiler_params=_cparams(1),
        name="final_norm",
    )(x, g)


def _rope_angles(pos, dim):
    inv = ROPE_THETA ** (-jnp.arange(0, dim, 2, dtype=F32) / dim)
    ang = pos.astype(F32)[:, None] * inv[None, :]
    return jnp.cos(ang), jnp.sin(ang)


def _rope_table(parts, s_len):
    cs, sn, sp = [], [], []
    for width, cos, sin, half in parts:
        if cos is None:
            cs.append(jnp.ones((s_len, width), F32))
            sn.append(jnp.zeros((s_len, width), F32))
            sp.append(jnp.zeros((s_len, width), F32))
            continue
        reps = width // (2 * half)
        zero = jnp.zeros_like(sin)
        cs.append(jnp.tile(jnp.concatenate([cos, cos], axis=-1), (1, reps)))
        sn.append(jnp.tile(jnp.concatenate([-sin, zero], axis=-1), (1, reps)))
        sp.append(jnp.tile(jnp.concatenate([zero, sin], axis=-1), (1, reps)))
    return jnp.stack([jnp.concatenate(cs, -1), jnp.concatenate(sn, -1), jnp.concatenate(sp, -1)])


def _deinterleave(a, stride):
    if stride == 1:
        return a
    b, h, s, d = a.shape
    return a.reshape(b, h, s // stride, stride, d).transpose(0, 3, 1, 2, 4).reshape(b * stride, h, s // stride, d)


def _reinterleave(a, stride, b):
    if stride == 1:
        return a
    _, l, w = a.shape
    return a.reshape(b, stride, l, w).transpose(0, 2, 1, 3).reshape(b, l * stride, w)


def kernel(x, g_mix, w_in, win_sink, ax_q_norm, ax_k_norm, mla_q_norm, mla_kv_norm, mla_w_uq, mla_w_ukv,
           g_group, w_out, g_ffn, peer_w_q, peer_sub_keys, peer_u, peer_v, g_final):
    b, s, d = x.shape
    t = b * s
    depth = w_in.shape[0]
    hd = HEAD_DIM

    pos = jnp.arange(s)
    cos1, sin1 = _rope_angles(pos, hd)
    cos_m, sin_m = _rope_angles(pos, MLA_ROPE)
    cos_r, sin_r = _rope_angles(pos // GRID_W, hd // 2)
    cos_c, sin_c = _rope_angles(pos % GRID_W, hd // 2)
    taba = _rope_table([(LANES, cos1, sin1, hd // 2)], s)
    tabb = _rope_table([(32, cos_r, sin_r, 16), (32, cos_c, sin_c, 16)] * 2, s)
    tabd = _rope_table([(MLA_NOPE, None, None, 0), (MLA_ROPE, cos_m, sin_m, 16), (32, None, None, 0)], s)
    head_mean = jnp.asarray(np.kron(np.eye(LANES // hd), np.full((hd, hd), 1.0 / hd)), F32)

    tm_prep = min(256, s)
    tm_tok = 256
    tm_gather = 128
    xf = x
    for l in range(depth):
        w_l = w_in[l]
        kr_cols = jnp.concatenate([jnp.zeros((d, MLA_NOPE), F32), w_l[:, 1920:1952], jnp.zeros((d, 32), F32)], -1)
        w_pad = jnp.concatenate([w_l[:, :1920], kr_cols], axis=-1).astype(BF16)
        wuq = mla_w_uq[l].reshape(-1, 4, MLA_NOPE + MLA_ROPE)
        wuq = jnp.pad(wuq, ((0, 0), (0, 0), (0, MLA_PAD - MLA_NOPE - MLA_ROPE))).reshape(-1, 4 * MLA_PAD).astype(BF16)
        wukv = mla_w_ukv[l].reshape(-1, 4, MLA_NOPE + hd)
        wuk = jnp.pad(wukv[:, :, :MLA_NOPE], ((0, 0), (0, 0), (0, MLA_PAD - MLA_NOPE))).reshape(-1, 4 * MLA_PAD).astype(BF16)
        wuv = wukv[:, :, MLA_NOPE:].reshape(-1, 4 * hd).astype(BF16)
        gq = jnp.tile(ax_q_norm[l], 2)[None]
        gk = jnp.tile(ax_k_norm[l], 2)[None]

        (qa, ka, va, qb, kb, vb, qc, kc, vc, qd, kd, vd) = _prep(
            xf.reshape(b, s, d), g_mix[l][None], w_pad, taba, tabb, tabd, gq, gk,
            mla_q_norm[l][None], mla_kv_norm[l][None], wuq, wuk, wuv, head_mean, tm_prep)

        oa = _banded(qa, ka, va, WIN_RADIUS, sink=win_sink[l])
        ob = _dense(qb, kb, vb, tm_prep)
        od = _dense(qd, kd, vd, tm_prep)
        dil = []
        for stride in DIL_STRIDES:
            o, lse = _banded(_deinterleave(qc, stride), _deinterleave(kc, stride), _deinterleave(vc, stride),
                             DIL_RADIUS)
            dil += [_reinterleave(o, stride, b).reshape(t, 4 * hd), _reinterleave(lse, stride, b).reshape(t, 4 * hd)]

        xf = _mix_out(xf.reshape(t, d), oa.reshape(t, 4 * hd), ob.reshape(t, 4 * hd), od.reshape(t, 4 * hd),
                      *dil, g_group[l], w_out[l].astype(BF16), tm_tok)

        hn, eid, gate = _peer_sel(xf, g_ffn[l][None], peer_w_q[l].astype(BF16),
                                  peer_sub_keys[l].astype(BF16), tm_tok)
        wgt = _peer_u(eid, _pack_table(peer_u[l]), hn.reshape(t, 8, LANES), gate, tm_gather)
        xf = _peer_v(eid, wgt, _pack_table(peer_v[l]), xf.reshape(t, 8, LANES), tm_gather).reshape(t, d)

    return _final_norm(xf, g_final[None], tm_tok).reshape(b, s, d)
```

```python
import functools
import math

import jax
import jax.numpy as jnp
import numpy as np
from jax import lax
from jax.experimental import pallas as pl
from jax.experimental.pallas import tpu as pltpu

F32 = jnp.float32
BF16 = jnp.bfloat16

EPS = 1e-6
NEG = -1e30
ROPE_THETA = 10000.0
HEAD_DIM = 64
GRID_W = 64
WIN_RADIUS = 128
DIL_RADIUS = 64
DIL_STRIDES = (1, 4, 16)
MLA_NOPE = 64
MLA_ROPE = 32
MLA_PAD = 128
PEER_KEYS = 128
PEER_HEADS = 8
PEER_TOPK = 16
PEER_SLOTS = PEER_HEADS * PEER_TOPK
IN_PAD = 2048

LANES = 128
ROW_WORDS = 4
PACK_ROWS = 512
VMEM_LIMIT = 56 * 1024 * 1024


def _cparams(n_axes, vmem=None):
    return pltpu.CompilerParams(
        dimension_semantics=("arbitrary",) * n_axes,
        vmem_limit_bytes=vmem or VMEM_LIMIT)


def _rms(x, g):
    return x * lax.rsqrt(jnp.mean(x * x, axis=-1, keepdims=True) + EPS) * g


def _rope(x, tab_ref, shift):
    n = x.shape[-1]
    return (x * tab_ref[0] + pltpu.roll(x, n - shift, 1) * tab_ref[1]
            + pltpu.roll(x, shift, 1) * tab_ref[2])


def _prep_kernel(x_ref, g_ref, w_ref, taba_ref, tabb_ref, tabd_ref, gq_ref, gk_ref,
                 gcq_ref, gckv_ref, wuq_ref, wuk_ref, wuv_ref, hm_ref,
                 qa, ka, va, qb, kb, vb, qc, kc, vc, qd, kd, vd, z_scr):
    h = _rms(x_ref[...], g_ref[...])
    z_scr[...] = jnp.dot(h.astype(BF16), w_ref[...], preferred_element_type=F32)
    hd = HEAD_DIM
    qscale = hd ** -0.5

    def put_heads(dst, first, val):
        for i in range(val.shape[-1] // hd):
            dst[first + i] = val[:, i * hd:(i + 1) * hd].astype(dst.dtype)

    def rotary_group(base, q_dst, k_dst, v_dst):
        for half in range(2):
            q = _rope(z_scr[:, base + half * LANES: base + (half + 1) * LANES], taba_ref, hd // 2)
            put_heads(q_dst, 2 * half, q * qscale)
        put_heads(k_dst, 0, _rope(z_scr[:, base + 256: base + 384], taba_ref, hd // 2))
        put_heads(v_dst, 0, z_scr[:, base + 384: base + 512])

    rotary_group(0, qa, ka, va)
    rotary_group(1024, qc, kc, vc)

    def head_norm(x, g):
        ms = jnp.dot(x * x, hm_ref[...], preferred_element_type=F32)
        return x * lax.rsqrt(ms + EPS) * g

    for half in range(2):
        q = head_norm(z_scr[:, 512 + half * LANES: 512 + (half + 1) * LANES], gq_ref[...])
        put_heads(qb, 2 * half, _rope(q, tabb_ref, hd // 4) * qscale)
    k = head_norm(z_scr[:, 768:896], gk_ref[...])
    put_heads(kb, 0, _rope(k, tabb_ref, hd // 4))
    put_heads(vb, 0, z_scr[:, 896:1024])

    cq = _rms(z_scr[:, 1536:1792], gcq_ref[...]).astype(BF16)
    qfull = jnp.dot(cq, wuq_ref[...], preferred_element_type=F32)
    dscale = (MLA_NOPE + MLA_ROPE) ** -0.5
    ckv = _rms(z_scr[:, 1792:1920], gckv_ref[...]).astype(BF16)
    kfull = jnp.dot(ckv, wuk_ref[...], preferred_element_type=F32)
    vfull = jnp.dot(ckv, wuv_ref[...], preferred_element_type=F32)
    kr = _rope(z_scr[:, 1920:2048], tabd_ref, MLA_ROPE // 2)
    for i in range(4):
        sl = slice(i * MLA_PAD, (i + 1) * MLA_PAD)
        qd[i] = (_rope(qfull[:, sl], tabd_ref, MLA_ROPE // 2) * dscale).astype(qd.dtype)
        kd[i] = (kfull[:, sl] + kr).astype(kd.dtype)
    put_heads(vd, 0, vfull)


def _prep(x, g, w_pad, taba, tabb, tabd, gq, gk, gcq, gckv, wuq, wuk, wuv, hm, tm):
    b, s, d = x.shape
    nt = s // tm
    full = lambda shape: pl.BlockSpec(shape, lambda bi, i: (0,) * len(shape))
    tab = pl.BlockSpec((3, tm, LANES), lambda bi, i: (0, i, 0))

    def hspec(nh, w):
        return pl.BlockSpec((None, nh, tm, w), lambda bi, i: (bi, 0, i, 0))

    def hshape(nh, w):
        return jax.ShapeDtypeStruct((b, nh, s, w), BF16)

    trio_specs = [hspec(4, 64), hspec(2, 64), hspec(2, 64)]
    trio_shapes = [hshape(4, 64), hshape(2, 64), hshape(2, 64)]
    return pl.pallas_call(
        _prep_kernel,
        grid=(b, nt),
        in_specs=[pl.BlockSpec((None, tm, d), lambda bi, i: (bi, i, 0)),
                  full((1, d)), full((d, IN_PAD)), tab, tab, tab,
                  full((1, LANES)), full((1, LANES)), full((1, 256)), full((1, LANES)),
                  full((256, 4 * MLA_PAD)), full((LANES, 4 * MLA_PAD)), full((LANES, 256)),
                  full((LANES, LANES))],
        out_specs=trio_specs * 3 + [hspec(4, MLA_PAD), hspec(4, MLA_PAD), hspec(4, 64)],
        out_shape=trio_shapes * 3 + [hshape(4, MLA_PAD), hshape(4, MLA_PAD), hshape(4, 64)],
        scratch_shapes=[pltpu.VMEM((tm, IN_PAD), F32)],
        compiler_params=_cparams(2),
        name="prep",
    )(x, g, w_pad, taba, tabb, tabd, gq, gk, gcq, gckv, wuq, wuk, wuv, hm)


def _banded_kernel(*refs, radius, tq, win, seq, group, with_sink):
    if with_sink:
        sink_ref, q_ref, k_ref, v_ref, o_ref = refs
    else:
        q_ref, k_ref, v_ref, o_ref, lse_ref = refs
    q0 = pl.program_id(1) * tq
    start = pl.multiple_of(jnp.clip(q0 - radius, 0, seq - win), 64)
    qpos = q0 + lax.broadcasted_iota(jnp.int32, (tq, win), 0)
    kpos = start + lax.broadcasted_iota(jnp.int32, (tq, win), 1)
    valid = jnp.abs(kpos - qpos) <= radius
    hd = HEAD_DIM
    for h in range(q_ref.shape[0]):
        k = k_ref[h // group, pl.ds(start, win), :]
        v = v_ref[h // group, pl.ds(start, win), :]
        s = lax.dot_general(q_ref[h], k, (((1,), (1,)), ((), ())), preferred_element_type=F32)
        s = jnp.where(valid, s, NEG)
        m = jnp.max(s, axis=-1, keepdims=True)
        if with_sink:
            sink = sink_ref[h]
            m = jnp.maximum(m, sink)
        e = jnp.exp(s - m)
        den = jnp.sum(e, axis=-1, keepdims=True)
        if with_sink:
            den = den + jnp.exp(sink - m)
        o = jnp.dot(e.astype(BF16), v, preferred_element_type=F32) / den
        o_ref[:, h * hd:(h + 1) * hd] = o
        if not with_sink:
            lse_ref[:, h * hd:(h + 1) * hd] = jnp.broadcast_to(m + jnp.log(den), (tq, hd))


def _banded(q, k, v, radius, sink=None):
    nb, nh, seq, hd = q.shape
    nkv = k.shape[1]
    tq = min(256, seq)
    win = min(seq, tq + 2 * radius)
    with_sink = sink is not None
    kern = functools.partial(_banded_kernel, radius=radius, tq=tq, win=win, seq=seq,
                             group=nh // nkv, with_sink=with_sink)
    qspec = pl.BlockSpec((None, nh, tq, hd), lambda b, i: (b, 0, i, 0))
    kvspec = pl.BlockSpec((None, nkv, seq, hd), lambda b, i: (b, 0, 0, 0))
    ospec = pl.BlockSpec((None, tq, nh * hd), lambda b, i: (b, i, 0))
    oshape = jax.ShapeDtypeStruct((nb, seq, nh * hd), F32)
    in_specs = [qspec, kvspec, kvspec]
    args = [q, k, v]
    if with_sink:
        in_specs = [pl.BlockSpec(memory_space=pltpu.SMEM)] + in_specs
        args = [sink] + args
    return pl.pallas_call(
        kern,
        grid=(nb, seq // tq),
        in_specs=in_specs,
        out_specs=ospec if with_sink else [ospec, ospec],
        out_shape=oshape if with_sink else [oshape, oshape],
        compiler_params=_cparams(2),
        name="banded_sink" if with_sink else "banded_stats",
    )(*args)


def _dense_kernel(q_ref, k_ref, v_ref, o_ref, *, group):
    hd = v_ref.shape[-1]
    for h in range(q_ref.shape[0]):
        s = lax.dot_general(q_ref[h], k_ref[h // group], (((1,), (1,)), ((), ())),
                            preferred_element_type=F32)
        m = jnp.max(s, axis=-1, keepdims=True)
        e = jnp.exp(s - m)
        den = jnp.sum(e, axis=-1, keepdims=True)
        o = jnp.dot(e.astype(BF16), v_ref[h // group], preferred_element_type=F32)
        o_ref[:, h * hd:(h + 1) * hd] = o / den


def _dense(q, k, v, tq):
    b, nh, s, dk = q.shape
    nkv = k.shape[1]
    hd = v.shape[-1]
    return pl.pallas_call(
        functools.partial(_dense_kernel, group=nh // nkv),
        grid=(b, s // tq),
        in_specs=[pl.BlockSpec((None, nh, tq, dk), lambda bi, i: (bi, 0, i, 0)),
                  pl.BlockSpec((None, nkv, s, dk), lambda bi, i: (bi, 0, 0, 0)),
                  pl.BlockSpec((None, nkv, s, hd), lambda bi, i: (bi, 0, 0, 0))],
        out_specs=pl.BlockSpec((None, tq, nh * hd), lambda bi, i: (bi, i, 0)),
        out_shape=jax.ShapeDtypeStruct((b, s, nh * hd), F32),
        compiler_params=_cparams(2),
        name="dense_attn",
    )(q, k, v)


def _mix_out_kernel(x_ref, oa_ref, ob_ref, od_ref, o1_ref, l1_ref, o4_ref, l4_ref, o16_ref, l16_ref,
                    gg_ref, w_ref, out_ref):
    l1, l4, l16 = l1_ref[...], l4_ref[...], l16_ref[...]
    lm = jnp.maximum(jnp.maximum(l1, l4), l16)
    w1, w4, w16 = jnp.exp(l1 - lm), jnp.exp(l4 - lm), jnp.exp(l16 - lm)
    oc = (w1 * o1_ref[...] + w4 * o4_ref[...] + w16 * o16_ref[...]) / (w1 + w4 + w16)
    gg = gg_ref[...]
    y = jnp.concatenate([_rms(oa_ref[...], gg[0:1]), _rms(ob_ref[...], gg[1:2]),
                         _rms(oc, gg[2:3]), _rms(od_ref[...], gg[3:4])], axis=-1)
    out_ref[...] = x_ref[...] + jnp.dot(y.astype(BF16), w_ref[...], preferred_element_type=F32)


def _mix_out(x, oa, ob, od, o1, l1, o4, l4, o16, l16, gg, w, tm):
    t, d = x.shape
    row = lambda w_: pl.BlockSpec((tm, w_), lambda i: (i, 0))
    full = lambda shape: pl.BlockSpec(shape, lambda i: (0,) * len(shape))
    return pl.pallas_call(
        _mix_out_kernel,
        grid=(t // tm,),
        in_specs=[row(d)] + [row(256)] * 9 + [full(gg.shape), full(w.shape)],
        out_specs=row(d),
        out_shape=jax.ShapeDtypeStruct((t, d), F32),
        compiler_params=_cparams(1),
        name="mix_out",
    )(x, oa, ob, od, o1, l1, o4, l4, o16, l16, gg, w)


def _topk_rows(vals, ids, k):
    best, picked = [], []
    for _ in range(k):
        m = jnp.max(vals, axis=0, keepdims=True)
        idx = jnp.min(jnp.where(vals == m, ids, jnp.float32(2 ** 30)), axis=0, keepdims=True)
        best.append(m)
        picked.append(idx)
        vals = jnp.where(ids == idx, -jnp.inf, vals)
    return jnp.concatenate(best, axis=0), jnp.concatenate(picked, axis=0)


_PAIR_LIMIT = tuple(PEER_TOPK // (i + 1) for i in range(PEER_TOPK))


def _select_experts(s1, i1, s2, i2):
    cols = s1.shape[1]
    row8 = lax.broadcasted_iota(jnp.int32, (8, cols), 0)
    row16 = lax.broadcasted_iota(jnp.int32, (PEER_TOPK, cols), 0).astype(F32)
    cand, flat = [s1[0:1] + s2], [row16]
    for i in range(1, 8):
        cand.append(jnp.where(row8 < _PAIR_LIMIT[i], s1[i:i + 1] + s2[0:8], -jnp.inf))
        flat.append((row8 + i * PEER_TOPK).astype(F32))
    cand.append(s1[8:16] + s2[0:1])
    flat.append(((row8 + 8) * PEER_TOPK).astype(F32))
    best, pick = _topk_rows(jnp.concatenate(cand, axis=0), jnp.concatenate(flat, axis=0), PEER_TOPK)
    pick = pick.astype(jnp.int32)
    pi, pj = pick // PEER_TOPK, pick % PEER_TOPK
    k1 = jnp.zeros(pick.shape, F32)
    k2 = jnp.zeros(pick.shape, F32)
    for r in range(PEER_TOPK):
        k1 = jnp.where(pi == r, i1[r:r + 1], k1)
        k2 = jnp.where(pj == r, i2[r:r + 1], k2)
    e = jnp.exp(best - best[0:1])
    gate = e / jnp.sum(e, axis=0, keepdims=True)
    odd_slot = lax.broadcasted_iota(jnp.int32, pick.shape, 0) % 2
    return gate, (k1 * PEER_KEYS + k2 + PACK_ROWS).astype(jnp.int32) * ROW_WORDS - odd_slot * ROW_WORDS


def _peer_sel_kernel(x_ref, g_ref, wq_ref, keys_ref, hn_ref, eid_ref, gate_ref, q_scr):
    hn = _rms(x_ref[...], g_ref[...])
    hn_ref[...] = hn
    q_scr[...] = jnp.dot(hn.astype(BF16), wq_ref[...], preferred_element_type=F32).astype(BF16)
    tm = x_ref.shape[0]
    key_ids = lax.broadcasted_iota(jnp.int32, (PEER_KEYS, LANES), 0).astype(F32)
    eids, gates = [], []
    for h in range(PEER_HEADS):
        sc = []
        for c in range(2):
            col = (2 * h + c) * LANES
            sc.append(lax.dot_general(keys_ref[h, c], q_scr[:, col:col + LANES],
                                      (((1,), (1,)), ((), ())), preferred_element_type=F32))
        eid_h, gate_h = [], []
        for part in range(tm // LANES):
            lanes = slice(part * LANES, (part + 1) * LANES)
            s1, i1 = _topk_rows(sc[0][:, lanes], key_ids, PEER_TOPK)
            s2, i2 = _topk_rows(sc[1][:, lanes], key_ids, PEER_TOPK)
            gate, eid = _select_experts(s1, i1, s2, i2)
            eid_h.append(eid)
            gate_h.append(gate)
        eids.append(jnp.concatenate(eid_h, axis=1))
        gates.append(jnp.concatenate(gate_h, axis=1))
    eid_ref[...] = jnp.concatenate(eids, axis=0).T
    gate_ref[...] = jnp.concatenate(gates, axis=0).T


def _peer_sel(x, g, wq, keys, tm):
    t, d = x.shape
    full = lambda shape: pl.BlockSpec(shape, lambda i: (0,) * len(shape))
    return pl.pallas_call(
        _peer_sel_kernel,
        grid=(t // tm,),
        in_specs=[pl.BlockSpec((tm, d), lambda i: (i, 0)), full((1, d)), full(wq.shape), full(keys.shape)],
        out_specs=[pl.BlockSpec((tm, d), lambda i: (i, 0)),
                   pl.BlockSpec((tm, PEER_SLOTS), lambda i: (i, 0)),
                   pl.BlockSpec((tm, PEER_SLOTS), lambda i: (i, 0))],
        out_shape=[jax.ShapeDtypeStruct((t, d), F32),
                   jax.ShapeDtypeStruct((t, PEER_SLOTS), jnp.int32),
                   jax.ShapeDtypeStruct((t, PEER_SLOTS), F32)],
        scratch_shapes=[pltpu.VMEM((tm, wq.shape[1]), BF16)],
        compiler_params=_cparams(1),
        name="peer_sel",
    )(x, g, wq, keys)


def _pack_kernel(t_ref, o_ref):
    step = pl.program_id(0)
    is_pad = jnp.logical_or(step == 0, step == pl.num_programs(0) - 1)

    @pl.when(is_pad)
    def _():
        o_ref[...] = jnp.zeros(o_ref.shape, o_ref.dtype)

    @pl.when(jnp.logical_not(is_pad))
    def _():
        x = t_ref[...]
        half = x.shape[1] // 2
        lo = pltpu.bitcast(x[:, :half].astype(BF16).astype(F32), jnp.uint32) >> 16
        hi = pltpu.bitcast(x[:, half:].astype(BF16).astype(F32), jnp.uint32) & jnp.uint32(0xFFFF0000)
        o_ref[...] = lo | hi


def _pack_table(tabs, layer):
    _, n, d = tabs.shape
    nb = n // PACK_ROWS
    words = pl.pallas_call(
        _pack_kernel,
        grid=(nb + 2,),
        in_specs=[pl.BlockSpec((None, PACK_ROWS, d), lambda i: (layer, jnp.clip(i - 1, 0, nb - 1), 0))],
        out_specs=pl.BlockSpec((PACK_ROWS, d // 2), lambda i: (i, 0)),
        out_shape=jax.ShapeDtypeStruct((n + 2 * PACK_ROWS, d // 2), jnp.uint32),
        compiler_params=_cparams(1),
        name="pack_table",
    )(tabs)
    return words.reshape((n + 2 * PACK_ROWS) * ROW_WORDS, LANES)


def _slot_masks():
    col = lax.broadcasted_iota(jnp.int32, (8, PEER_SLOTS * 8), 1) % 8
    row = lax.broadcasted_iota(jnp.int32, (8, PEER_SLOTS * 8), 0)
    return col == 2 * (row % ROW_WORDS) + row // ROW_WORDS


def _split_bf16(x):
    hi = x.astype(BF16)
    return hi, (x - hi.astype(F32)).astype(BF16)


def _gather_rows(eid_ref, tab_ref, t):
    low_half = lax.broadcasted_iota(jnp.int32, (8, LANES), 0) < ROW_WORDS
    pairs = []
    for j in range(0, PEER_SLOTS, 2):
        even = pl.multiple_of(eid_ref[t, j], ROW_WORDS)
        odd = pl.multiple_of(eid_ref[t, j + 1], ROW_WORDS)
        pairs.append(jnp.where(low_half, tab_ref[pl.ds(even, 8), :], tab_ref[pl.ds(odd, 8), :]))
    return pltpu.bitcast(jnp.concatenate(pairs, axis=0), BF16)


def _peer_u_kernel(eid_ref, tab_ref, hn_ref, gate_ref, w_ref, r_scr):
    tm = hn_ref.shape[0]
    own = _slot_masks()

    for t in range(tm):
        rows = _gather_rows(eid_ref, tab_ref, t)
        h_hi, h_lo = _split_bf16(hn_ref[t])
        d = lax.dot_general(jnp.concatenate([h_hi, h_lo], axis=0), rows, (((1,), (1,)), ((), ())),
                            preferred_element_type=F32)
        d = d[0:8] + d[8:16]
        r_scr[pl.ds(t, 1), :] = jnp.sum(jnp.where(own, d, 0.0), axis=0, keepdims=True)

    fold = (lax.broadcasted_iota(jnp.int32, (PEER_SLOTS * 8, PEER_SLOTS), 0) // 8
            == lax.broadcasted_iota(jnp.int32, (PEER_SLOTS * 8, PEER_SLOTS), 1)).astype(BF16)
    r_hi, r_lo = _split_bf16(r_scr[...])
    a = jnp.dot(r_hi, fold, preferred_element_type=F32) + jnp.dot(r_lo, fold, preferred_element_type=F32)
    act = 0.5 * a * (1.0 + lax.erf(a * (2.0 ** -0.5)))
    w_ref[...] = gate_ref[...] * act


def _peer_u(eid, table, hn3, gate, tm):
    t = hn3.shape[0]
    return pl.pallas_call(
        _peer_u_kernel,
        grid=(t // tm,),
        in_specs=[pl.BlockSpec((tm, PEER_SLOTS), lambda i: (i, 0), memory_space=pltpu.SMEM,
                               pipeline_mode=pl.Buffered(1)),
                  pl.BlockSpec(memory_space=pltpu.VMEM),
                  pl.BlockSpec((tm, 8, LANES), lambda i: (i, 0, 0)),
                  pl.BlockSpec((tm, PEER_SLOTS), lambda i: (i, 0))],
        out_specs=pl.BlockSpec((tm, PEER_SLOTS), lambda i: (i, 0)),
        out_shape=jax.ShapeDtypeStruct((t, PEER_SLOTS), F32),
        scratch_shapes=[pltpu.VMEM((tm, PEER_SLOTS * 8), F32)],
        compiler_params=_cparams(1),
        name="peer_u",
    )(eid, table, hn3, gate)


def _peer_v_kernel(eid_ref, w_ref, tab_ref, x_ref, out_ref, whi_scr, wlo_scr):
    tm = x_ref.shape[0]
    own = _slot_masks()
    spread = (lax.broadcasted_iota(jnp.int32, (PEER_SLOTS, PEER_SLOTS * 8), 1) // 8
              == lax.broadcasted_iota(jnp.int32, (PEER_SLOTS, PEER_SLOTS * 8), 0)).astype(BF16)
    w_hi, w_lo = _split_bf16(w_ref[...])
    whi_scr[...] = jnp.dot(w_hi, spread, preferred_element_type=F32)
    wlo_scr[...] = jnp.dot(w_lo, spread, preferred_element_type=F32)

    for t in range(tm):
        rows = _gather_rows(eid_ref, tab_ref, t)
        lhs = jnp.concatenate([jnp.where(own, whi_scr[pl.ds(t, 1), :], 0.0),
                               jnp.where(own, wlo_scr[pl.ds(t, 1), :], 0.0)], axis=0).astype(BF16)
        o = jnp.dot(lhs, rows, preferred_element_type=F32)
        out_ref[t] = x_ref[t] + o[0:8] + o[8:16]


def _peer_v(eid, w, table, x3, tm):
    t = x3.shape[0]
    return pl.pallas_call(
        _peer_v_kernel,
        grid=(t // tm,),
        in_specs=[pl.BlockSpec((tm, PEER_SLOTS), lambda i: (i, 0), memory_space=pltpu.SMEM,
                               pipeline_mode=pl.Buffered(1)),
                  pl.BlockSpec((tm, PEER_SLOTS), lambda i: (i, 0)),
                  pl.BlockSpec(memory_space=pltpu.VMEM),
                  pl.BlockSpec((tm, 8, LANES), lambda i: (i, 0, 0))],
        out_specs=pl.BlockSpec((tm, 8, LANES), lambda i: (i, 0, 0)),
        out_shape=jax.ShapeDtypeStruct(x3.shape, F32),
        scratch_shapes=[pltpu.VMEM((tm, PEER_SLOTS * 8), F32),
                        pltpu.VMEM((tm, PEER_SLOTS * 8), F32)],
        compiler_params=_cparams(1),
        name="peer_v",
    )(eid, w, table, x3)


def _final_norm_kernel(x_ref, g_ref, o_ref):
    o_ref[...] = _rms(x_ref[...], g_ref[...])


def _final_norm(x, g, tm):
    t, d = x.shape
    return pl.pallas_call(
        _final_norm_kernel,
        grid=(t // tm,),
        in_specs=[pl.BlockSpec((tm, d), lambda i: (i, 0)), pl.BlockSpec((1, d), lambda i: (0, 0))],
        out_specs=pl.BlockSpec((tm, d), lambda i: (i, 0)),
        out_shape=jax.ShapeDtypeStruct((t, d), F32),
        compiler_params=_cparams(1),
        name="final_norm",
    )(x, g)


def _rope_angles(pos, dim):
    inv = ROPE_THETA ** (-jnp.arange(0, dim, 2, dtype=F32) / dim)
    ang = pos.astype(F32)[:, None] * inv[None, :]
    return jnp.cos(ang), jnp.sin(ang)


def _rope_table(parts, s_len):
    cs, sn, sp = [], [], []
    for width, cos, sin, half in parts:
        if cos is None:
            cs.append(jnp.ones((s_len, width), F32))
            sn.append(jnp.zeros((s_len, width), F32))
            sp.append(jnp.zeros((s_len, width), F32))
            continue
        reps = width // (2 * half)
        zero = jnp.zeros_like(sin)
        cs.append(jnp.tile(jnp.concatenate([cos, cos], axis=-1), (1, reps)))
        sn.append(jnp.tile(jnp.concatenate([-sin, zero], axis=-1), (1, reps)))
        sp.append(jnp.tile(jnp.concatenate([zero, sin], axis=-1), (1, reps)))
    return jnp.stack([jnp.concatenate(cs, -1), jnp.concatenate(sn, -1), jnp.concatenate(sp, -1)])


def _deinterleave(a, stride):
    if stride == 1:
        return a
    b, h, s, d = a.shape
    return a.reshape(b, h, s // stride, stride, d).transpose(0, 3, 1, 2, 4).reshape(b * stride, h, s // stride, d)


def _reinterleave(a, stride, b):
    if stride == 1:
        return a
    _, l, w = a.shape
    return a.reshape(b, stride, l, w).transpose(0, 2, 1, 3).reshape(b, l * stride, w)


def kernel(x, g_mix, w_in, win_sink, ax_q_norm, ax_k_norm, mla_q_norm, mla_kv_norm, mla_w_uq, mla_w_ukv,
           g_group, w_out, g_ffn, peer_w_q, peer_sub_keys, peer_u, peer_v, g_final):
    b, s, d = x.shape
    t = b * s
    depth = w_in.shape[0]
    hd = HEAD_DIM

    pos = jnp.arange(s)
    cos1, sin1 = _rope_angles(pos, hd)
    cos_m, sin_m = _rope_angles(pos, MLA_ROPE)
    cos_r, sin_r = _rope_angles(pos // GRID_W, hd // 2)
    cos_c, sin_c = _rope_angles(pos % GRID_W, hd // 2)
    taba = _rope_table([(LANES, cos1, sin1, hd // 2)], s)
    tabb = _rope_table([(32, cos_r, sin_r, 16), (32, cos_c, sin_c, 16)] * 2, s)
    tabd = _rope_table([(MLA_NOPE, None, None, 0), (MLA_ROPE, cos_m, sin_m, 16), (32, None, None, 0)], s)
    head_mean = jnp.asarray(np.kron(np.eye(LANES // hd), np.full((hd, hd), 1.0 / hd)), F32)

    tm_prep = min(256, s)
    tm_tok = 256
    tm_gather = 256
    xf = x
    for l in range(depth):
        w_l = w_in[l]
        kr_cols = jnp.concatenate([jnp.zeros((d, MLA_NOPE), F32), w_l[:, 1920:1952], jnp.zeros((d, 32), F32)], -1)
        w_pad = jnp.concatenate([w_l[:, :1920], kr_cols], axis=-1).astype(BF16)
        wuq = mla_w_uq[l].reshape(-1, 4, MLA_NOPE + MLA_ROPE)
        wuq = jnp.pad(wuq, ((0, 0), (0, 0), (0, MLA_PAD - MLA_NOPE - MLA_ROPE))).reshape(-1, 4 * MLA_PAD).astype(BF16)
        wukv = mla_w_ukv[l].reshape(-1, 4, MLA_NOPE + hd)
        wuk = jnp.pad(wukv[:, :, :MLA_NOPE], ((0, 0), (0, 0), (0, MLA_PAD - MLA_NOPE))).reshape(-1, 4 * MLA_PAD).astype(BF16)
        wuv = wukv[:, :, MLA_NOPE:].reshape(-1, 4 * hd).astype(BF16)
        gq = jnp.tile(ax_q_norm[l], 2)[None]
        gk = jnp.tile(ax_k_norm[l], 2)[None]

        (qa, ka, va, qb, kb, vb, qc, kc, vc, qd, kd, vd) = _prep(
            xf.reshape(b, s, d), g_mix[l][None], w_pad, taba, tabb, tabd, gq, gk,
            mla_q_norm[l][None], mla_kv_norm[l][None], wuq, wuk, wuv, head_mean, tm_prep)

        oa = _banded(qa, ka, va, WIN_RADIUS, sink=win_sink[l])
        ob = _dense(qb, kb, vb, tm_prep)
        od = _dense(qd, kd, vd, tm_prep)
        dil = []
        for stride in DIL_STRIDES:
            o, lse = _banded(_deinterleave(qc, stride), _deinterleave(kc, stride), _deinterleave(vc, stride),
                             DIL_RADIUS)
            dil += [_reinterleave(o, stride, b).reshape(t, 4 * hd), _reinterleave(lse, stride, b).reshape(t, 4 * hd)]

        xf = _mix_out(xf.reshape(t, d), oa.reshape(t, 4 * hd), ob.reshape(t, 4 * hd), od.reshape(t, 4 * hd),
                      *dil, g_group[l], w_out[l].astype(BF16), tm_tok)

        hn, eid, gate = _peer_sel(xf, g_ffn[l][None], peer_w_q[l].astype(BF16),
                                  peer_sub_keys[l].astype(BF16), tm_tok)
        wgt = _peer_u(eid, _pack_table(peer_u, l), hn.reshape(t, 8, LANES), gate, tm_gather)
        xf = _peer_v(eid, wgt, _pack_table(peer_v, l), xf.reshape(t, 8, LANES), tm_gather).reshape(t, d)

    return _final_norm(xf, g_final[None], tm_tok).reshape(b, s, d)
```

```python
import functools
import math

import jax
import jax.numpy as jnp
import numpy as np
from jax import lax
from jax.experimental import pallas as pl
from jax.experimental.pallas import tpu as pltpu

F32 = jnp.float32
BF16 = jnp.bfloat16

EPS = 1e-6
NEG = -1e30
ROPE_THETA = 10000.0
HEAD_DIM = 64
GRID_W = 64
WIN_RADIUS = 128
DIL_RADIUS = 64
DIL_STRIDES = (1, 4, 16)
MLA_NOPE = 64
MLA_ROPE = 32
MLA_PAD = 128
PEER_KEYS = 128
PEER_HEADS = 8
PEER_TOPK = 16
PEER_SLOTS = PEER_HEADS * PEER_TOPK
IN_PAD = 2048

LANES = 128
ROW_WORDS = 4
PACK_ROWS = 512
VMEM_LIMIT = 56 * 1024 * 1024


def _cparams(n_axes, vmem=None):
    return pltpu.CompilerParams(
        dimension_semantics=("arbitrary",) * n_axes,
        vmem_limit_bytes=vmem or VMEM_LIMIT)


def _rms(x, g):
    return x * lax.rsqrt(jnp.mean(x * x, axis=-1, keepdims=True) + EPS) * g


def _rope(x, tab_ref, shift):
    n = x.shape[-1]
    return (x * tab_ref[0] + pltpu.roll(x, n - shift, 1) * tab_ref[1]
            + pltpu.roll(x, shift, 1) * tab_ref[2])


def _prep_kernel(x_ref, g_ref, w_ref, taba_ref, tabb_ref, tabd_ref, gq_ref, gk_ref,
                 gcq_ref, gckv_ref, wuq_ref, wuk_ref, wuv_ref, hm_ref,
                 qa, ka, va, qb, kb, vb, qc, kc, vc, qc4, kc4, vc4, qc16, kc16, vc16, qd, kd, vd,
                 z_scr, c_scr):
    h = _rms(x_ref[...], g_ref[...])
    z_scr[...] = jnp.dot(h.astype(BF16), w_ref[...], preferred_element_type=F32)
    hd = HEAD_DIM
    qscale = hd ** -0.5

    def put_heads(dst, first, val):
        for i in range(val.shape[-1] // hd):
            dst[first + i] = val[:, i * hd:(i + 1) * hd].astype(dst.dtype)

    def rotary_group(base, q_dst, k_dst, v_dst):
        for half in range(2):
            q = _rope(z_scr[:, base + half * LANES: base + (half + 1) * LANES], taba_ref, hd // 2)
            put_heads(q_dst, 2 * half, q * qscale)
        put_heads(k_dst, 0, _rope(z_scr[:, base + 256: base + 384], taba_ref, hd // 2))
        put_heads(v_dst, 0, z_scr[:, base + 384: base + 512])

    rotary_group(0, qa, ka, va)

    for half in range(2):
        q = _rope(z_scr[:, 1024 + half * LANES: 1024 + (half + 1) * LANES], taba_ref, hd // 2)
        c_scr[half] = q * qscale
    c_scr[2] = _rope(z_scr[:, 1280:1408], taba_ref, hd // 2)
    c_scr[3] = z_scr[:, 1408:1536]
    rows = c_scr.shape[1]

    def emit(dsts, take):
        q_dst, k_dst, v_dst = dsts
        put_heads(q_dst, 0, take(0))
        put_heads(q_dst, 2, take(1))
        put_heads(k_dst, 0, take(2))
        put_heads(v_dst, 0, take(3))

    emit((qc, kc, vc), lambda c: c_scr[c])
    for stride, dsts in zip(DIL_STRIDES[1:], ((qc4, kc4, vc4), (qc16, kc16, vc16))):
        for p in range(stride):
            emit([dst.at[p] for dst in dsts],
                 lambda c: c_scr[c, pl.ds(p, rows // stride, stride=stride), :])

    def head_norm(x, g):
        ms = jnp.dot(x * x, hm_ref[...], preferred_element_type=F32)
        return x * lax.rsqrt(ms + EPS) * g

    for half in range(2):
        q = head_norm(z_scr[:, 512 + half * LANES: 512 + (half + 1) * LANES], gq_ref[...])
        put_heads(qb, 2 * half, _rope(q, tabb_ref, hd // 4) * qscale)
    k = head_norm(z_scr[:, 768:896], gk_ref[...])
    put_heads(kb, 0, _rope(k, tabb_ref, hd // 4))
    put_heads(vb, 0, z_scr[:, 896:1024])

    cq = _rms(z_scr[:, 1536:1792], gcq_ref[...]).astype(BF16)
    qfull = jnp.dot(cq, wuq_ref[...], preferred_element_type=F32)
    dscale = (MLA_NOPE + MLA_ROPE) ** -0.5
    ckv = _rms(z_scr[:, 1792:1920], gckv_ref[...]).astype(BF16)
    kfull = jnp.dot(ckv, wuk_ref[...], preferred_element_type=F32)
    vfull = jnp.dot(ckv, wuv_ref[...], preferred_element_type=F32)
    kr = _rope(z_scr[:, 1920:2048], tabd_ref, MLA_ROPE // 2)
    for i in range(4):
        sl = slice(i * MLA_PAD, (i + 1) * MLA_PAD)
        qd[i] = (_rope(qfull[:, sl], tabd_ref, MLA_ROPE // 2) * dscale).astype(qd.dtype)
        kd[i] = (kfull[:, sl] + kr).astype(kd.dtype)
    put_heads(vd, 0, vfull)


def _prep(x, g, w_pad, taba, tabb, tabd, gq, gk, gcq, gckv, wuq, wuk, wuv, hm, tm):
    b, s, d = x.shape
    nt = s // tm
    full = lambda shape: pl.BlockSpec(shape, lambda bi, i: (0,) * len(shape))
    tab = pl.BlockSpec((3, tm, LANES), lambda bi, i: (0, i, 0))

    def hspec(nh, w):
        return pl.BlockSpec((None, nh, tm, w), lambda bi, i: (bi, 0, i, 0))

    def hshape(nh, w):
        return jax.ShapeDtypeStruct((b, nh, s, w), BF16)

    trio_specs = [hspec(4, 64), hspec(2, 64), hspec(2, 64)]
    trio_shapes = [hshape(4, 64), hshape(2, 64), hshape(2, 64)]
    strided_specs, strided_shapes = [], []
    for r in DIL_STRIDES[1:]:
        for nh in (4, 2, 2):
            strided_specs.append(pl.BlockSpec((None, r, nh, tm // r, 64), lambda bi, i: (bi, 0, 0, i, 0)))
            strided_shapes.append(jax.ShapeDtypeStruct((b, r, nh, s // r, 64), BF16))
    return pl.pallas_call(
        _prep_kernel,
        grid=(b, nt),
        in_specs=[pl.BlockSpec((None, tm, d), lambda bi, i: (bi, i, 0)),
                  full((1, d)), full((d, IN_PAD)), tab, tab, tab,
                  full((1, LANES)), full((1, LANES)), full((1, 256)), full((1, LANES)),
                  full((256, 4 * MLA_PAD)), full((LANES, 4 * MLA_PAD)), full((LANES, 256)),
                  full((LANES, LANES))],
        out_specs=trio_specs * 3 + strided_specs + [hspec(4, MLA_PAD), hspec(4, MLA_PAD), hspec(4, 64)],
        out_shape=trio_shapes * 3 + strided_shapes + [hshape(4, MLA_PAD), hshape(4, MLA_PAD), hshape(4, 64)],
        scratch_shapes=[pltpu.VMEM((tm, IN_PAD), F32), pltpu.VMEM((4, tm, LANES), F32)],
        compiler_params=_cparams(2),
        name="prep",
    )(x, g, w_pad, taba, tabb, tabd, gq, gk, gcq, gckv, wuq, wuk, wuv, hm)


def _banded_kernel(*refs, radius, tq, win, seq, group, with_sink):
    if with_sink:
        sink_ref, q_ref, k_ref, v_ref, o_ref = refs
    else:
        q_ref, k_ref, v_ref, o_ref, lse_ref = refs
    q0 = pl.program_id(1) * tq
    start = pl.multiple_of(jnp.clip(q0 - radius, 0, seq - win), 64)
    qpos = q0 + lax.broadcasted_iota(jnp.int32, (tq, win), 0)
    kpos = start + lax.broadcasted_iota(jnp.int32, (tq, win), 1)
    valid = jnp.abs(kpos - qpos) <= radius
    hd = HEAD_DIM
    for h in range(q_ref.shape[0]):
        k = k_ref[h // group, pl.ds(start, win), :]
        v = v_ref[h // group, pl.ds(start, win), :]
        s = lax.dot_general(q_ref[h], k, (((1,), (1,)), ((), ())), preferred_element_type=F32)
        s = jnp.where(valid, s, NEG)
        m = jnp.max(s, axis=-1, keepdims=True)
        if with_sink:
            sink = sink_ref[h]
            m = jnp.maximum(m, sink)
        e = jnp.exp(s - m)
        den = jnp.sum(e, axis=-1, keepdims=True)
        if with_sink:
            den = den + jnp.exp(sink - m)
        o = jnp.dot(e.astype(BF16), v, preferred_element_type=F32) / den
        o_ref[:, h * hd:(h + 1) * hd] = o
        if not with_sink:
            lse_ref[:, h * hd:(h + 1) * hd] = jnp.broadcast_to(m + jnp.log(den), (tq, hd))


def _banded(q, k, v, radius, sink=None):
    nb, nh, seq, hd = q.shape
    nkv = k.shape[1]
    tq = min(256, seq)
    win = min(seq, tq + 2 * radius)
    with_sink = sink is not None
    kern = functools.partial(_banded_kernel, radius=radius, tq=tq, win=win, seq=seq,
                             group=nh // nkv, with_sink=with_sink)
    qspec = pl.BlockSpec((None, nh, tq, hd), lambda b, i: (b, 0, i, 0))
    kvspec = pl.BlockSpec((None, nkv, seq, hd), lambda b, i: (b, 0, 0, 0))
    ospec = pl.BlockSpec((None, tq, nh * hd), lambda b, i: (b, i, 0))
    oshape = jax.ShapeDtypeStruct((nb, seq, nh * hd), F32)
    in_specs = [qspec, kvspec, kvspec]
    args = [q, k, v]
    if with_sink:
        in_specs = [pl.BlockSpec(memory_space=pltpu.SMEM)] + in_specs
        args = [sink] + args
    return pl.pallas_call(
        kern,
        grid=(nb, seq // tq),
        in_specs=in_specs,
        out_specs=ospec if with_sink else [ospec, ospec],
        out_shape=oshape if with_sink else [oshape, oshape],
        compiler_params=_cparams(2),
        name="banded_sink" if with_sink else "banded_stats",
    )(*args)


def _dense_kernel(q_ref, k_ref, v_ref, o_ref, *, group):
    hd = v_ref.shape[-1]
    for h in range(q_ref.shape[0]):
        s = lax.dot_general(q_ref[h], k_ref[h // group], (((1,), (1,)), ((), ())),
                            preferred_element_type=F32)
        m = jnp.max(s, axis=-1, keepdims=True)
        e = jnp.exp(s - m)
        den = jnp.sum(e, axis=-1, keepdims=True)
        o = jnp.dot(e.astype(BF16), v_ref[h // group], preferred_element_type=F32)
        o_ref[:, h * hd:(h + 1) * hd] = o / den


def _dense(q, k, v, tq):
    b, nh, s, dk = q.shape
    nkv = k.shape[1]
    hd = v.shape[-1]
    return pl.pallas_call(
        functools.partial(_dense_kernel, group=nh // nkv),
        grid=(b, s // tq),
        in_specs=[pl.BlockSpec((None, nh, tq, dk), lambda bi, i: (bi, 0, i, 0)),
                  pl.BlockSpec((None, nkv, s, dk), lambda bi, i: (bi, 0, 0, 0)),
                  pl.BlockSpec((None, nkv, s, hd), lambda bi, i: (bi, 0, 0, 0))],
        out_specs=pl.BlockSpec((None, tq, nh * hd), lambda bi, i: (bi, i, 0)),
        out_shape=jax.ShapeDtypeStruct((b, s, nh * hd), F32),
        compiler_params=_cparams(2),
        name="dense_attn",
    )(q, k, v)


def _mix_out_kernel(x_ref, oa_ref, ob_ref, od_ref, o1_ref, l1_ref, o4_ref, l4_ref, o16_ref, l16_ref,
                    gg_ref, w_ref, out_ref, t_scr):
    def token_order(ref):
        stride, part = ref.shape[0], ref.shape[1]
        for p in range(stride):
            for c in range(2):
                t_scr[c, pl.ds(p, part, stride=stride), :] = ref[p, :, c * LANES:(c + 1) * LANES]
        return jnp.concatenate([t_scr[0], t_scr[1]], axis=-1)

    l1, l4, l16 = l1_ref[...], token_order(l4_ref), token_order(l16_ref)
    lm = jnp.maximum(jnp.maximum(l1, l4), l16)
    w1, w4, w16 = jnp.exp(l1 - lm), jnp.exp(l4 - lm), jnp.exp(l16 - lm)
    oc = (w1 * o1_ref[...] + w4 * token_order(o4_ref) + w16 * token_order(o16_ref)) / (w1 + w4 + w16)
    gg = gg_ref[...]
    y = jnp.concatenate([_rms(oa_ref[...], gg[0:1]), _rms(ob_ref[...], gg[1:2]),
                         _rms(oc, gg[2:3]), _rms(od_ref[...], gg[3:4])], axis=-1)
    out_ref[...] = x_ref[...] + jnp.dot(y.astype(BF16), w_ref[...], preferred_element_type=F32)


def _mix_out(x, oa, ob, od, o1, l1, o4, l4, o16, l16, gg, w, tm, batch):
    t, d = x.shape
    nt = t // batch // tm
    row = lambda w_: pl.BlockSpec((tm, w_), lambda bi, i: (bi * nt + i, 0))
    full = lambda shape: pl.BlockSpec(shape, lambda bi, i: (0,) * len(shape))
    strided = lambda r: pl.BlockSpec((None, r, tm // r, 256), lambda bi, i: (bi, 0, i, 0))
    return pl.pallas_call(
        _mix_out_kernel,
        grid=(batch, nt),
        in_specs=[row(d)] + [row(256)] * 5 + [strided(4)] * 2 + [strided(16)] * 2 + [full(gg.shape), full(w.shape)],
        out_specs=row(d),
        out_shape=jax.ShapeDtypeStruct((t, d), F32),
        scratch_shapes=[pltpu.VMEM((2, tm, LANES), F32)],
        compiler_params=_cparams(2),
        name="mix_out",
    )(x, oa, ob, od, o1, l1, o4, l4, o16, l16, gg, w)


def _topk_rows(vals, ids, k):
    best, picked = [], []
    for _ in range(k):
        m = jnp.max(vals, axis=0, keepdims=True)
        idx = jnp.min(jnp.where(vals == m, ids, jnp.float32(2 ** 30)), axis=0, keepdims=True)
        best.append(m)
        picked.append(idx)
        vals = jnp.where(ids == idx, -jnp.inf, vals)
    return jnp.concatenate(best, axis=0), jnp.concatenate(picked, axis=0)


_PAIR_LIMIT = tuple(PEER_TOPK // (i + 1) for i in range(PEER_TOPK))


def _select_experts(s1, i1, s2, i2):
    cols = s1.shape[1]
    row8 = lax.broadcasted_iota(jnp.int32, (8, cols), 0)
    row16 = lax.broadcasted_iota(jnp.int32, (PEER_TOPK, cols), 0).astype(F32)
    cand, flat = [s1[0:1] + s2], [row16]
    for i in range(1, 8):
        cand.append(jnp.where(row8 < _PAIR_LIMIT[i], s1[i:i + 1] + s2[0:8], -jnp.inf))
        flat.append((row8 + i * PEER_TOPK).astype(F32))
    cand.append(s1[8:16] + s2[0:1])
    flat.append(((row8 + 8) * PEER_TOPK).astype(F32))
    best, pick = _topk_rows(jnp.concatenate(cand, axis=0), jnp.concatenate(flat, axis=0), PEER_TOPK)
    pick = pick.astype(jnp.int32)
    pi, pj = pick // PEER_TOPK, pick % PEER_TOPK
    k1 = jnp.zeros(pick.shape, F32)
    k2 = jnp.zeros(pick.shape, F32)
    for r in range(PEER_TOPK):
        k1 = jnp.where(pi == r, i1[r:r + 1], k1)
        k2 = jnp.where(pj == r, i2[r:r + 1], k2)
    e = jnp.exp(best - best[0:1])
    gate = e / jnp.sum(e, axis=0, keepdims=True)
    odd_slot = lax.broadcasted_iota(jnp.int32, pick.shape, 0) % 2
    return gate, (k1 * PEER_KEYS + k2 + PACK_ROWS).astype(jnp.int32) * ROW_WORDS - odd_slot * ROW_WORDS


def _peer_sel_kernel(x_ref, g_ref, wq_ref, keys_ref, hn_ref, eid_ref, gate_ref, q_scr):
    hn = _rms(x_ref[...], g_ref[...])
    hn_ref[...] = hn
    q_scr[...] = jnp.dot(hn.astype(BF16), wq_ref[...], preferred_element_type=F32).astype(BF16)
    tm = x_ref.shape[0]
    key_ids = lax.broadcasted_iota(jnp.int32, (PEER_KEYS, LANES), 0).astype(F32)
    eids, gates = [], []
    for h in range(PEER_HEADS):
        sc = []
        for c in range(2):
            col = (2 * h + c) * LANES
            sc.append(lax.dot_general(keys_ref[h, c], q_scr[:, col:col + LANES],
                                      (((1,), (1,)), ((), ())), preferred_element_type=F32))
        eid_h, gate_h = [], []
        for part in range(tm // LANES):
            lanes = slice(part * LANES, (part + 1) * LANES)
            s1, i1 = _topk_rows(sc[0][:, lanes], key_ids, PEER_TOPK)
            s2, i2 = _topk_rows(sc[1][:, lanes], key_ids, PEER_TOPK)
            gate, eid = _select_experts(s1, i1, s2, i2)
            eid_h.append(eid)
            gate_h.append(gate)
        eids.append(jnp.concatenate(eid_h, axis=1))
        gates.append(jnp.concatenate(gate_h, axis=1))
    eid_ref[...] = jnp.concatenate(eids, axis=0).T
    gate_ref[...] = jnp.concatenate(gates, axis=0).T


def _peer_sel(x, g, wq, keys, tm):
    t, d = x.shape
    full = lambda shape: pl.BlockSpec(shape, lambda i: (0,) * len(shape))
    return pl.pallas_call(
        _peer_sel_kernel,
        grid=(t // tm,),
        in_specs=[pl.BlockSpec((tm, d), lambda i: (i, 0)), full((1, d)), full(wq.shape), full(keys.shape)],
        out_specs=[pl.BlockSpec((tm, d), lambda i: (i, 0)),
                   pl.BlockSpec((tm, PEER_SLOTS), lambda i: (i, 0)),
                   pl.BlockSpec((tm, PEER_SLOTS), lambda i: (i, 0))],
        out_shape=[jax.ShapeDtypeStruct((t, d), F32),
                   jax.ShapeDtypeStruct((t, PEER_SLOTS), jnp.int32),
                   jax.ShapeDtypeStruct((t, PEER_SLOTS), F32)],
        scratch_shapes=[pltpu.VMEM((tm, wq.shape[1]), BF16)],
        compiler_params=_cparams(1),
        name="peer_sel",
    )(x, g, wq, keys)


def _pack_kernel(t_ref, o_ref):
    step = pl.program_id(0)
    is_pad = jnp.logical_or(step == 0, step == pl.num_programs(0) - 1)

    @pl.when(is_pad)
    def _():
        o_ref[...] = jnp.zeros(o_ref.shape, o_ref.dtype)

    @pl.when(jnp.logical_not(is_pad))
    def _():
        x = t_ref[...]
        half = x.shape[1] // 2
        lo = pltpu.bitcast(x[:, :half].astype(BF16).astype(F32), jnp.uint32) >> 16
        hi = pltpu.bitcast(x[:, half:].astype(BF16).astype(F32), jnp.uint32) & jnp.uint32(0xFFFF0000)
        o_ref[...] = lo | hi


def _pack_table(tabs, layer):
    _, n, d = tabs.shape
    nb = n // PACK_ROWS
    words = pl.pallas_call(
        _pack_kernel,
        grid=(nb + 2,),
        in_specs=[pl.BlockSpec((None, PACK_ROWS, d), lambda i: (layer, jnp.clip(i - 1, 0, nb - 1), 0))],
        out_specs=pl.BlockSpec((PACK_ROWS, d // 2), lambda i: (i, 0)),
        out_shape=jax.ShapeDtypeStruct((n + 2 * PACK_ROWS, d // 2), jnp.uint32),
        compiler_params=_cparams(1),
        name="pack_table",
    )(tabs)
    return words.reshape((n + 2 * PACK_ROWS) * ROW_WORDS, LANES)


def _slot_masks():
    col = lax.broadcasted_iota(jnp.int32, (8, PEER_SLOTS * 8), 1) % 8
    row = lax.broadcasted_iota(jnp.int32, (8, PEER_SLOTS * 8), 0)
    return col == 2 * (row % ROW_WORDS) + row // ROW_WORDS


def _split_bf16(x):
    hi = x.astype(BF16)
    return hi, (x - hi.astype(F32)).astype(BF16)


def _gather_rows(eid_ref, tab_ref, t):
    low_half = lax.broadcasted_iota(jnp.int32, (8, LANES), 0) < ROW_WORDS
    pairs = []
    for j in range(0, PEER_SLOTS, 2):
        even = pl.multiple_of(eid_ref[t, j], ROW_WORDS)
        odd = pl.multiple_of(eid_ref[t, j + 1], ROW_WORDS)
        pairs.append(jnp.where(low_half, tab_ref[pl.ds(even, 8), :], tab_ref[pl.ds(odd, 8), :]))
    return pltpu.bitcast(jnp.concatenate(pairs, axis=0), BF16)


def _peer_u_kernel(eid_ref, tab_ref, hn_ref, gate_ref, w_ref, r_scr):
    tm = hn_ref.shape[0]
    own = _slot_masks()

    for t in range(tm):
        rows = _gather_rows(eid_ref, tab_ref, t)
        h_hi, h_lo = _split_bf16(hn_ref[t])
        d = lax.dot_general(jnp.concatenate([h_hi, h_lo], axis=0), rows, (((1,), (1,)), ((), ())),
                            preferred_element_type=F32)
        d = d[0:8] + d[8:16]
        r_scr[pl.ds(t, 1), :] = jnp.sum(jnp.where(own, d, 0.0), axis=0, keepdims=True)

    fold = (lax.broadcasted_iota(jnp.int32, (PEER_SLOTS * 8, PEER_SLOTS), 0) // 8
            == lax.broadcasted_iota(jnp.int32, (PEER_SLOTS * 8, PEER_SLOTS), 1)).astype(BF16)
    r_hi, r_lo = _split_bf16(r_scr[...])
    a = jnp.dot(r_hi, fold, preferred_element_type=F32) + jnp.dot(r_lo, fold, preferred_element_type=F32)
    act = 0.5 * a * (1.0 + lax.erf(a * (2.0 ** -0.5)))
    w_ref[...] = gate_ref[...] * act


def _peer_u(eid, table, hn3, gate, tm):
    t = hn3.shape[0]
    return pl.pallas_call(
        _peer_u_kernel,
        grid=(t // tm,),
        in_specs=[pl.BlockSpec((tm, PEER_SLOTS), lambda i: (i, 0), memory_space=pltpu.SMEM,
                               pipeline_mode=pl.Buffered(1)),
                  pl.BlockSpec(memory_space=pltpu.VMEM),
                  pl.BlockSpec((tm, 8, LANES), lambda i: (i, 0, 0)),
                  pl.BlockSpec((tm, PEER_SLOTS), lambda i: (i, 0))],
        out_specs=pl.BlockSpec((tm, PEER_SLOTS), lambda i: (i, 0)),
        out_shape=jax.ShapeDtypeStruct((t, PEER_SLOTS), F32),
        scratch_shapes=[pltpu.VMEM((tm, PEER_SLOTS * 8), F32)],
        compiler_params=_cparams(1),
        name="peer_u",
    )(eid, table, hn3, gate)


def _peer_v_kernel(eid_ref, w_ref, tab_ref, x_ref, out_ref, whi_scr, wlo_scr):
    tm = x_ref.shape[0]
    own = _slot_masks()
    spread = (lax.broadcasted_iota(jnp.int32, (PEER_SLOTS, PEER_SLOTS * 8), 1) // 8
              == lax.broadcasted_iota(jnp.int32, (PEER_SLOTS, PEER_SLOTS * 8), 0)).astype(BF16)
    w_hi, w_lo = _split_bf16(w_ref[...])
    whi_scr[...] = jnp.dot(w_hi, spread, preferred_element_type=F32)
    wlo_scr[...] = jnp.dot(w_lo, spread, preferred_element_type=F32)

    for t in range(tm):
        rows = _gather_rows(eid_ref, tab_ref, t)
        lhs = jnp.concatenate([jnp.where(own, whi_scr[pl.ds(t, 1), :], 0.0),
                               jnp.where(own, wlo_scr[pl.ds(t, 1), :], 0.0)], axis=0).astype(BF16)
        o = jnp.dot(lhs, rows, preferred_element_type=F32)
        out_ref[t] = x_ref[t] + o[0:8] + o[8:16]


def _peer_v(eid, w, table, x3, tm):
    t = x3.shape[0]
    return pl.pallas_call(
        _peer_v_kernel,
        grid=(t // tm,),
        in_specs=[pl.BlockSpec((tm, PEER_SLOTS), lambda i: (i, 0), memory_space=pltpu.SMEM,
                               pipeline_mode=pl.Buffered(1)),
                  pl.BlockSpec((tm, PEER_SLOTS), lambda i: (i, 0)),
                  pl.BlockSpec(memory_space=pltpu.VMEM),
                  pl.BlockSpec((tm, 8, LANES), lambda i: (i, 0, 0))],
        out_specs=pl.BlockSpec((tm, 8, LANES), lambda i: (i, 0, 0)),
        out_shape=jax.ShapeDtypeStruct(x3.shape, F32),
        scratch_shapes=[pltpu.VMEM((tm, PEER_SLOTS * 8), F32),
                        pltpu.VMEM((tm, PEER_SLOTS * 8), F32)],
        compiler_params=_cparams(1),
        name="peer_v",
    )(eid, w, table, x3)


def _final_norm_kernel(x_ref, g_ref, o_ref):
    o_ref[...] = _rms(x_ref[...], g_ref[...])


def _final_norm(x, g, tm):
    t, d = x.shape
    return pl.pallas_call(
        _final_norm_kernel,
        grid=(t // tm,),
        in_specs=[pl.BlockSpec((tm, d), lambda i: (i, 0)), pl.BlockSpec((1, d), lambda i: (0, 0))],
        out_specs=pl.BlockSpec((tm, d), lambda i: (i, 0)),
        out_shape=jax.ShapeDtypeStruct((t, d), F32),
        compiler_params=_cparams(1),
        name="final_norm",
    )(x, g)


def _rope_angles(pos, dim):
    inv = ROPE_THETA ** (-jnp.arange(0, dim, 2, dtype=F32) / dim)
    ang = pos.astype(F32)[:, None] * inv[None, :]
    return jnp.cos(ang), jnp.sin(ang)


def _rope_table(parts, s_len):
    cs, sn, sp = [], [], []
    for width, cos, sin, half in parts:
        if cos is None:
            cs.append(jnp.ones((s_len, width), F32))
            sn.append(jnp.zeros((s_len, width), F32))
            sp.append(jnp.zeros((s_len, width), F32))
            continue
        reps = width // (2 * half)
        zero = jnp.zeros_like(sin)
        cs.append(jnp.tile(jnp.concatenate([cos, cos], axis=-1), (1, reps)))
        sn.append(jnp.tile(jnp.concatenate([-sin, zero], axis=-1), (1, reps)))
        sp.append(jnp.tile(jnp.concatenate([zero, sin], axis=-1), (1, reps)))
    return jnp.stack([jnp.concatenate(cs, -1), jnp.concatenate(sn, -1), jnp.concatenate(sp, -1)])


def kernel(x, g_mix, w_in, win_sink, ax_q_norm, ax_k_norm, mla_q_norm, mla_kv_norm, mla_w_uq, mla_w_ukv,
           g_group, w_out, g_ffn, peer_w_q, peer_sub_keys, peer_u, peer_v, g_final):
    b, s, d = x.shape
    t = b * s
    depth = w_in.shape[0]
    hd = HEAD_DIM

    pos = jnp.arange(s)
    cos1, sin1 = _rope_angles(pos, hd)
    cos_m, sin_m = _rope_angles(pos, MLA_ROPE)
    cos_r, sin_r = _rope_angles(pos // GRID_W, hd // 2)
    cos_c, sin_c = _rope_angles(pos % GRID_W, hd // 2)
    taba = _rope_table([(LANES, cos1, sin1, hd // 2)], s)
    tabb = _rope_table([(32, cos_r, sin_r, 16), (32, cos_c, sin_c, 16)] * 2, s)
    tabd = _rope_table([(MLA_NOPE, None, None, 0), (MLA_ROPE, cos_m, sin_m, 16), (32, None, None, 0)], s)
    head_mean = jnp.asarray(np.kron(np.eye(LANES // hd), np.full((hd, hd), 1.0 / hd)), F32)

    tm_prep = min(256, s)
    tm_tok = 256
    tm_gather = 256
    xf = x
    for l in range(depth):
        w_l = w_in[l]
        kr_cols = jnp.concatenate([jnp.zeros((d, MLA_NOPE), F32), w_l[:, 1920:1952], jnp.zeros((d, 32), F32)], -1)
        w_pad = jnp.concatenate([w_l[:, :1920], kr_cols], axis=-1).astype(BF16)
        wuq = mla_w_uq[l].reshape(-1, 4, MLA_NOPE + MLA_ROPE)
        wuq = jnp.pad(wuq, ((0, 0), (0, 0), (0, MLA_PAD - MLA_NOPE - MLA_ROPE))).reshape(-1, 4 * MLA_PAD).astype(BF16)
        wukv = mla_w_ukv[l].reshape(-1, 4, MLA_NOPE + hd)
        wuk = jnp.pad(wukv[:, :, :MLA_NOPE], ((0, 0), (0, 0), (0, MLA_PAD - MLA_NOPE))).reshape(-1, 4 * MLA_PAD).astype(BF16)
        wuv = wukv[:, :, MLA_NOPE:].reshape(-1, 4 * hd).astype(BF16)
        gq = jnp.tile(ax_q_norm[l], 2)[None]
        gk = jnp.tile(ax_k_norm[l], 2)[None]

        (qa, ka, va, qb, kb, vb, qc, kc, vc, qc4, kc4, vc4, qc16, kc16, vc16, qd, kd, vd) = _prep(
            xf.reshape(b, s, d), g_mix[l][None], w_pad, taba, tabb, tabd, gq, gk,
            mla_q_norm[l][None], mla_kv_norm[l][None], wuq, wuk, wuv, head_mean, tm_prep)

        oa = _banded(qa, ka, va, WIN_RADIUS, sink=win_sink[l])
        ob = _dense(qb, kb, vb, tm_prep)
        od = _dense(qd, kd, vd, tm_prep)
        dil = [a.reshape(t, 4 * hd) for a in _banded(qc, kc, vc, DIL_RADIUS)]
        for stride, (q_r, k_r, v_r) in zip(DIL_STRIDES[1:], ((qc4, kc4, vc4), (qc16, kc16, vc16))):
            seqs = lambda a: a.reshape((b * stride,) + a.shape[2:])
            o, lse = _banded(seqs(q_r), seqs(k_r), seqs(v_r), DIL_RADIUS)
            dil += [o.reshape(b, stride, s // stride, 4 * hd), lse.reshape(b, stride, s // stride, 4 * hd)]

        xf = _mix_out(xf.reshape(t, d), oa.reshape(t, 4 * hd), ob.reshape(t, 4 * hd), od.reshape(t, 4 * hd),
                      *dil, g_group[l], w_out[l].astype(BF16), tm_tok, b)

        hn, eid, gate = _peer_sel(xf, g_ffn[l][None], peer_w_q[l].astype(BF16),
                                  peer_sub_keys[l].astype(BF16), tm_tok)
        wgt = _peer_u(eid, _pack_table(peer_u, l), hn.reshape(t, 8, LANES), gate, tm_gather)
        xf = _peer_v(eid, wgt, _pack_table(peer_v, l), xf.reshape(t, 8, LANES), tm_gather).reshape(t, d)

    return _final_norm(xf, g_final[None], tm_tok).reshape(b, s, d)
```

```python
import functools
import math

import jax
import jax.numpy as jnp
import numpy as np
from jax import lax
from jax.experimental import pallas as pl
from jax.experimental.pallas import tpu as pltpu

F32 = jnp.float32
BF16 = jnp.bfloat16

EPS = 1e-6
NEG = -1e30
ROPE_THETA = 10000.0
HEAD_DIM = 64
GRID_W = 64
WIN_RADIUS = 128
DIL_RADIUS = 64
DIL_STRIDES = (1, 4, 16)
MLA_NOPE = 64
MLA_ROPE = 32
MLA_PAD = 128
PEER_KEYS = 128
PEER_HEADS = 8
PEER_TOPK = 16
PEER_SLOTS = PEER_HEADS * PEER_TOPK
IN_PAD = 2048

LANES = 128
ROW_WORDS = 4
PACK_ROWS = 512
VMEM_LIMIT = 56 * 1024 * 1024


def _cparams(n_axes, vmem=None):
    return pltpu.CompilerParams(
        dimension_semantics=("arbitrary",) * n_axes,
        vmem_limit_bytes=vmem or VMEM_LIMIT)


def _rms(x, g):
    return x * lax.rsqrt(jnp.mean(x * x, axis=-1, keepdims=True) + EPS) * g


def _rope(x, tab_ref, shift):
    n = x.shape[-1]
    return (x * tab_ref[0] + pltpu.roll(x, n - shift, 1) * tab_ref[1]
            + pltpu.roll(x, shift, 1) * tab_ref[2])


def _prep_kernel(x_ref, g_ref, w_ref, taba_ref, tabb_ref, tabd_ref, gq_ref, gk_ref,
                 gcq_ref, gckv_ref, wuq_ref, wuk_ref, wuv_ref, hm_ref,
                 qa, ka, va, qb, kb, vb, qc, kc, vc, qc4, kc4, vc4, qc16, kc16, vc16, qd, kd, vd,
                 z_scr, c_scr):
    h = _rms(x_ref[...], g_ref[...])
    z_scr[...] = jnp.dot(h.astype(BF16), w_ref[...], preferred_element_type=F32)
    hd = HEAD_DIM
    qscale = hd ** -0.5

    def put_heads(dst, first, val):
        for i in range(val.shape[-1] // hd):
            dst[first + i] = val[:, i * hd:(i + 1) * hd].astype(dst.dtype)

    def rotary_group(base, q_dst, k_dst, v_dst):
        for half in range(2):
            q = _rope(z_scr[:, base + half * LANES: base + (half + 1) * LANES], taba_ref, hd // 2)
            put_heads(q_dst, 2 * half, q * qscale)
        put_heads(k_dst, 0, _rope(z_scr[:, base + 256: base + 384], taba_ref, hd // 2))
        put_heads(v_dst, 0, z_scr[:, base + 384: base + 512])

    rotary_group(0, qa, ka, va)

    for half in range(2):
        q = _rope(z_scr[:, 1024 + half * LANES: 1024 + (half + 1) * LANES], taba_ref, hd // 2)
        c_scr[half] = q * qscale
    c_scr[2] = _rope(z_scr[:, 1280:1408], taba_ref, hd // 2)
    c_scr[3] = z_scr[:, 1408:1536]
    rows = c_scr.shape[1]

    def emit(dsts, take):
        q_dst, k_dst, v_dst = dsts
        put_heads(q_dst, 0, take(0))
        put_heads(q_dst, 2, take(1))
        put_heads(k_dst, 0, take(2))
        put_heads(v_dst, 0, take(3))

    emit((qc, kc, vc), lambda c: c_scr[c])
    for stride, dsts in zip(DIL_STRIDES[1:], ((qc4, kc4, vc4), (qc16, kc16, vc16))):
        for p in range(stride):
            emit([dst.at[p] for dst in dsts],
                 lambda c: c_scr[c, pl.ds(p, rows // stride, stride=stride), :])

    def head_norm(x, g):
        ms = jnp.dot(x * x, hm_ref[...], preferred_element_type=F32)
        return x * lax.rsqrt(ms + EPS) * g

    for half in range(2):
        q = head_norm(z_scr[:, 512 + half * LANES: 512 + (half + 1) * LANES], gq_ref[...])
        put_heads(qb, 2 * half, _rope(q, tabb_ref, hd // 4) * qscale)
    k = head_norm(z_scr[:, 768:896], gk_ref[...])
    put_heads(kb, 0, _rope(k, tabb_ref, hd // 4))
    put_heads(vb, 0, z_scr[:, 896:1024])

    cq = _rms(z_scr[:, 1536:1792], gcq_ref[...]).astype(BF16)
    qfull = jnp.dot(cq, wuq_ref[...], preferred_element_type=F32)
    dscale = (MLA_NOPE + MLA_ROPE) ** -0.5
    ckv = _rms(z_scr[:, 1792:1920], gckv_ref[...]).astype(BF16)
    kfull = jnp.dot(ckv, wuk_ref[...], preferred_element_type=F32)
    vfull = jnp.dot(ckv, wuv_ref[...], preferred_element_type=F32)
    kr = _rope(z_scr[:, 1920:2048], tabd_ref, MLA_ROPE // 2)
    for i in range(4):
        sl = slice(i * MLA_PAD, (i + 1) * MLA_PAD)
        qd[i] = (_rope(qfull[:, sl], tabd_ref, MLA_ROPE // 2) * dscale).astype(qd.dtype)
        kd[i] = (kfull[:, sl] + kr).astype(kd.dtype)
    put_heads(vd, 0, vfull)


def _prep(x, g, w_pad, taba, tabb, tabd, gq, gk, gcq, gckv, wuq, wuk, wuv, hm, tm):
    b, s, d = x.shape
    nt = s // tm
    full = lambda shape: pl.BlockSpec(shape, lambda bi, i: (0,) * len(shape))
    tab = pl.BlockSpec((3, tm, LANES), lambda bi, i: (0, i, 0))

    def hspec(nh, w):
        return pl.BlockSpec((None, nh, tm, w), lambda bi, i: (bi, 0, i, 0))

    def hshape(nh, w):
        return jax.ShapeDtypeStruct((b, nh, s, w), BF16)

    trio_specs = [hspec(4, 64), hspec(2, 64), hspec(2, 64)]
    trio_shapes = [hshape(4, 64), hshape(2, 64), hshape(2, 64)]
    strided_specs, strided_shapes = [], []
    for r in DIL_STRIDES[1:]:
        for nh in (4, 2, 2):
            strided_specs.append(pl.BlockSpec((None, r, nh, tm // r, 64), lambda bi, i: (bi, 0, 0, i, 0)))
            strided_shapes.append(jax.ShapeDtypeStruct((b, r, nh, s // r, 64), BF16))
    return pl.pallas_call(
        _prep_kernel,
        grid=(b, nt),
        in_specs=[pl.BlockSpec((None, tm, d), lambda bi, i: (bi, i, 0)),
                  full((1, d)), full((d, IN_PAD)), tab, tab, tab,
                  full((1, LANES)), full((1, LANES)), full((1, 256)), full((1, LANES)),
                  full((256, 4 * MLA_PAD)), full((LANES, 4 * MLA_PAD)), full((LANES, 256)),
                  full((LANES, LANES))],
        out_specs=trio_specs * 3 + strided_specs + [hspec(4, MLA_PAD), hspec(4, MLA_PAD), hspec(4, 64)],
        out_shape=trio_shapes * 3 + strided_shapes + [hshape(4, MLA_PAD), hshape(4, MLA_PAD), hshape(4, 64)],
        scratch_shapes=[pltpu.VMEM((tm, IN_PAD), F32), pltpu.VMEM((4, tm, LANES), F32)],
        compiler_params=_cparams(2),
        name="prep",
    )(x, g, w_pad, taba, tabb, tabd, gq, gk, gcq, gckv, wuq, wuk, wuv, hm)


def _banded_kernel(*refs, radius, tq, win, seq, group, with_sink):
    if with_sink:
        sink_ref, q_ref, k_ref, v_ref, o_ref = refs
    else:
        q_ref, k_ref, v_ref, o_ref, lse_ref = refs
    q0 = pl.program_id(1) * tq
    start = pl.multiple_of(jnp.clip(q0 - radius, 0, seq - win), 64)
    qpos = q0 + lax.broadcasted_iota(jnp.int32, (tq, win), 0)
    kpos = start + lax.broadcasted_iota(jnp.int32, (tq, win), 1)
    valid = jnp.abs(kpos - qpos) <= radius
    hd = HEAD_DIM
    for h in range(q_ref.shape[0]):
        k = k_ref[h // group, pl.ds(start, win), :]
        v = v_ref[h // group, pl.ds(start, win), :]
        s = lax.dot_general(q_ref[h], k, (((1,), (1,)), ((), ())), preferred_element_type=F32)
        s = jnp.where(valid, s, NEG)
        m = jnp.max(s, axis=-1, keepdims=True)
        if with_sink:
            sink = sink_ref[h]
            m = jnp.maximum(m, sink)
        e = jnp.exp(s - m)
        den = jnp.sum(e, axis=-1, keepdims=True)
        if with_sink:
            den = den + jnp.exp(sink - m)
        o = jnp.dot(e.astype(BF16), v, preferred_element_type=F32) / den
        o_ref[:, h * hd:(h + 1) * hd] = o
        if not with_sink:
            lse_ref[:, h * hd:(h + 1) * hd] = jnp.broadcast_to(m + jnp.log(den), (tq, hd))


def _banded(q, k, v, radius, sink=None):
    nb, nh, seq, hd = q.shape
    nkv = k.shape[1]
    tq = min(256, seq)
    win = min(seq, tq + 2 * radius)
    with_sink = sink is not None
    kern = functools.partial(_banded_kernel, radius=radius, tq=tq, win=win, seq=seq,
                             group=nh // nkv, with_sink=with_sink)
    qspec = pl.BlockSpec((None, nh, tq, hd), lambda b, i: (b, 0, i, 0))
    kvspec = pl.BlockSpec((None, nkv, seq, hd), lambda b, i: (b, 0, 0, 0))
    ospec = pl.BlockSpec((None, tq, nh * hd), lambda b, i: (b, i, 0))
    oshape = jax.ShapeDtypeStruct((nb, seq, nh * hd), F32)
    in_specs = [qspec, kvspec, kvspec]
    args = [q, k, v]
    if with_sink:
        in_specs = [pl.BlockSpec(memory_space=pltpu.SMEM)] + in_specs
        args = [sink] + args
    return pl.pallas_call(
        kern,
        grid=(nb, seq // tq),
        in_specs=in_specs,
        out_specs=ospec if with_sink else [ospec, ospec],
        out_shape=oshape if with_sink else [oshape, oshape],
        compiler_params=_cparams(2),
        name="banded_sink" if with_sink else "banded_stats",
    )(*args)


def _dense_kernel(q_ref, k_ref, v_ref, o_ref, *, group):
    hd = v_ref.shape[-1]
    for h in range(q_ref.shape[0]):
        s = lax.dot_general(q_ref[h], k_ref[h // group], (((1,), (1,)), ((), ())),
                            preferred_element_type=F32)
        m = jnp.max(s, axis=-1, keepdims=True)
        e = jnp.exp(s - m)
        den = jnp.sum(e, axis=-1, keepdims=True)
        o = jnp.dot(e.astype(BF16), v_ref[h // group], preferred_element_type=F32)
        o_ref[:, h * hd:(h + 1) * hd] = o / den


def _dense(q, k, v, tq):
    b, nh, s, dk = q.shape
    nkv = k.shape[1]
    hd = v.shape[-1]
    return pl.pallas_call(
        functools.partial(_dense_kernel, group=nh // nkv),
        grid=(b, s // tq),
        in_specs=[pl.BlockSpec((None, nh, tq, dk), lambda bi, i: (bi, 0, i, 0)),
                  pl.BlockSpec((None, nkv, s, dk), lambda bi, i: (bi, 0, 0, 0)),
                  pl.BlockSpec((None, nkv, s, hd), lambda bi, i: (bi, 0, 0, 0))],
        out_specs=pl.BlockSpec((None, tq, nh * hd), lambda bi, i: (bi, i, 0)),
        out_shape=jax.ShapeDtypeStruct((b, s, nh * hd), F32),
        compiler_params=_cparams(2),
        name="dense_attn",
    )(q, k, v)


def _mix_out_kernel(x_ref, oa_ref, ob_ref, od_ref, o1_ref, l1_ref, o4_ref, l4_ref, o16_ref, l16_ref,
                    gg_ref, w_ref, out_ref, t_scr):
    def token_order(ref):
        stride, part = ref.shape[0], ref.shape[1]
        for p in range(stride):
            for c in range(2):
                t_scr[c, pl.ds(p, part, stride=stride), :] = ref[p, :, c * LANES:(c + 1) * LANES]
        return jnp.concatenate([t_scr[0], t_scr[1]], axis=-1)

    l1, l4, l16 = l1_ref[...], token_order(l4_ref), token_order(l16_ref)
    lm = jnp.maximum(jnp.maximum(l1, l4), l16)
    w1, w4, w16 = jnp.exp(l1 - lm), jnp.exp(l4 - lm), jnp.exp(l16 - lm)
    oc = (w1 * o1_ref[...] + w4 * token_order(o4_ref) + w16 * token_order(o16_ref)) / (w1 + w4 + w16)
    gg = gg_ref[...]
    y = jnp.concatenate([_rms(oa_ref[...], gg[0:1]), _rms(ob_ref[...], gg[1:2]),
                         _rms(oc, gg[2:3]), _rms(od_ref[...], gg[3:4])], axis=-1)
    out_ref[...] = x_ref[...] + jnp.dot(y.astype(BF16), w_ref[...], preferred_element_type=F32)


def _mix_out(x, oa, ob, od, o1, l1, o4, l4, o16, l16, gg, w, tm, batch):
    t, d = x.shape
    nt = t // batch // tm
    row = lambda w_: pl.BlockSpec((tm, w_), lambda bi, i: (bi * nt + i, 0))
    full = lambda shape: pl.BlockSpec(shape, lambda bi, i: (0,) * len(shape))
    strided = lambda r: pl.BlockSpec((None, r, tm // r, 256), lambda bi, i: (bi, 0, i, 0))
    return pl.pallas_call(
        _mix_out_kernel,
        grid=(batch, nt),
        in_specs=[row(d)] + [row(256)] * 5 + [strided(4)] * 2 + [strided(16)] * 2 + [full(gg.shape), full(w.shape)],
        out_specs=row(d),
        out_shape=jax.ShapeDtypeStruct((t, d), F32),
        scratch_shapes=[pltpu.VMEM((2, tm, LANES), F32)],
        compiler_params=_cparams(2),
        name="mix_out",
    )(x, oa, ob, od, o1, l1, o4, l4, o16, l16, gg, w)


def _topk_rows(vals, ids, k):
    best, picked = [], []
    for _ in range(k):
        m = jnp.max(vals, axis=0, keepdims=True)
        idx = jnp.min(jnp.where(vals == m, ids, jnp.float32(2 ** 30)), axis=0, keepdims=True)
        best.append(m)
        picked.append(idx)
        vals = jnp.where(ids == idx, -jnp.inf, vals)
    return jnp.concatenate(best, axis=0), jnp.concatenate(picked, axis=0)


_PAIR_LIMIT = tuple(PEER_TOPK // (i + 1) for i in range(PEER_TOPK))


def _select_experts(s1, i1, s2, i2):
    cols = s1.shape[1]
    row8 = lax.broadcasted_iota(jnp.int32, (8, cols), 0)
    row16 = lax.broadcasted_iota(jnp.int32, (PEER_TOPK, cols), 0).astype(F32)
    cand, flat = [s1[0:1] + s2], [row16]
    for i in range(1, 8):
        cand.append(jnp.where(row8 < _PAIR_LIMIT[i], s1[i:i + 1] + s2[0:8], -jnp.inf))
        flat.append((row8 + i * PEER_TOPK).astype(F32))
    cand.append(s1[8:16] + s2[0:1])
    flat.append(((row8 + 8) * PEER_TOPK).astype(F32))
    best, pick = _topk_rows(jnp.concatenate(cand, axis=0), jnp.concatenate(flat, axis=0), PEER_TOPK)
    pick = pick.astype(jnp.int32)
    pi, pj = pick // PEER_TOPK, pick % PEER_TOPK
    k1 = jnp.zeros(pick.shape, F32)
    k2 = jnp.zeros(pick.shape, F32)
    for r in range(PEER_TOPK):
        k1 = jnp.where(pi == r, i1[r:r + 1], k1)
        k2 = jnp.where(pj == r, i2[r:r + 1], k2)
    e = jnp.exp(best - best[0:1])
    gate = e / jnp.sum(e, axis=0, keepdims=True)
    odd_slot = lax.broadcasted_iota(jnp.int32, pick.shape, 0) % 2
    return gate, (k1 * PEER_KEYS + k2 + PACK_ROWS).astype(jnp.int32) * ROW_WORDS - odd_slot * ROW_WORDS


def _peer_sel_kernel(x_ref, g_ref, wq_ref, keys_ref, hn_ref, eid_ref, gate_ref, q_scr):
    hn = _rms(x_ref[...], g_ref[...])
    hn_ref[...] = hn
    q_scr[...] = jnp.dot(hn.astype(BF16), wq_ref[...], preferred_element_type=F32).astype(BF16)
    tm = x_ref.shape[0]
    key_ids = lax.broadcasted_iota(jnp.int32, (PEER_KEYS, LANES), 0).astype(F32)
    eids, gates = [], []
    for h in range(PEER_HEADS):
        sc = []
        for c in range(2):
            col = (2 * h + c) * LANES
            sc.append(lax.dot_general(keys_ref[h, c], q_scr[:, col:col + LANES],
                                      (((1,), (1,)), ((), ())), preferred_element_type=F32))
        eid_h, gate_h = [], []
        for part in range(tm // LANES):
            lanes = slice(part * LANES, (part + 1) * LANES)
            s1, i1 = _topk_rows(sc[0][:, lanes], key_ids, PEER_TOPK)
            s2, i2 = _topk_rows(sc[1][:, lanes], key_ids, PEER_TOPK)
            gate, eid = _select_experts(s1, i1, s2, i2)
            eid_h.append(eid)
            gate_h.append(gate)
        eids.append(jnp.concatenate(eid_h, axis=1))
        gates.append(jnp.concatenate(gate_h, axis=1))
    eid_ref[...] = jnp.concatenate(eids, axis=0).T
    gate_ref[...] = jnp.concatenate(gates, axis=0).T


def _peer_sel(x, g, wq, keys, tm):
    t, d = x.shape
    full = lambda shape: pl.BlockSpec(shape, lambda i: (0,) * len(shape))
    return pl.pallas_call(
        _peer_sel_kernel,
        grid=(t // tm,),
        in_specs=[pl.BlockSpec((tm, d), lambda i: (i, 0)), full((1, d)), full(wq.shape), full(keys.shape)],
        out_specs=[pl.BlockSpec((tm, d), lambda i: (i, 0)),
                   pl.BlockSpec((tm, PEER_SLOTS), lambda i: (i, 0)),
                   pl.BlockSpec((tm, PEER_SLOTS), lambda i: (i, 0))],
        out_shape=[jax.ShapeDtypeStruct((t, d), F32),
                   jax.ShapeDtypeStruct((t, PEER_SLOTS), jnp.int32),
                   jax.ShapeDtypeStruct((t, PEER_SLOTS), F32)],
        scratch_shapes=[pltpu.VMEM((tm, wq.shape[1]), BF16)],
        compiler_params=_cparams(1),
        name="peer_sel",
    )(x, g, wq, keys)


def _pack_kernel(t_ref, o_ref):
    step = pl.program_id(0)
    is_pad = jnp.logical_or(step == 0, step == pl.num_programs(0) - 1)

    @pl.when(is_pad)
    def _():
        o_ref[...] = jnp.zeros(o_ref.shape, o_ref.dtype)

    @pl.when(jnp.logical_not(is_pad))
    def _():
        x = t_ref[...]
        half = x.shape[1] // 2
        lo = pltpu.bitcast(x[:, :half].astype(BF16).astype(F32), jnp.uint32) >> 16
        hi = pltpu.bitcast(x[:, half:].astype(BF16).astype(F32), jnp.uint32) & jnp.uint32(0xFFFF0000)
        o_ref[...] = lo | hi


def _pack_table(tabs, layer):
    _, n, d = tabs.shape
    nb = n // PACK_ROWS
    words = pl.pallas_call(
        _pack_kernel,
        grid=(nb + 2,),
        in_specs=[pl.BlockSpec((None, PACK_ROWS, d), lambda i: (layer, jnp.clip(i - 1, 0, nb - 1), 0))],
        out_specs=pl.BlockSpec((PACK_ROWS, d // 2), lambda i: (i, 0)),
        out_shape=jax.ShapeDtypeStruct((n + 2 * PACK_ROWS, d // 2), jnp.uint32),
        compiler_params=_cparams(1),
        name="pack_table",
    )(tabs)
    return words.reshape((n + 2 * PACK_ROWS) * ROW_WORDS, LANES)


def _slot_masks():
    col = lax.broadcasted_iota(jnp.int32, (8, PEER_SLOTS * 8), 1) % 8
    row = lax.broadcasted_iota(jnp.int32, (8, PEER_SLOTS * 8), 0)
    return col == 2 * (row % ROW_WORDS) + row // ROW_WORDS


def _split_bf16(x):
    hi = x.astype(BF16)
    return hi, (x - hi.astype(F32)).astype(BF16)


def _gather_rows(eid_ref, tab_ref, t):
    low_half = lax.broadcasted_iota(jnp.int32, (8, LANES), 0) < ROW_WORDS
    pairs = []
    for j in range(0, PEER_SLOTS, 2):
        even = pl.multiple_of(eid_ref[t, j], ROW_WORDS)
        odd = pl.multiple_of(eid_ref[t, j + 1], ROW_WORDS)
        pairs.append(jnp.where(low_half, tab_ref[pl.ds(even, 8), :], tab_ref[pl.ds(odd, 8), :]))
    return pltpu.bitcast(jnp.concatenate(pairs, axis=0), BF16)


def _peer_u_kernel(eid_ref, tab_ref, hn_ref, gate_ref, w_ref, r_scr):
    tm = hn_ref.shape[0]
    own = _slot_masks()

    for t in range(tm):
        rows = _gather_rows(eid_ref, tab_ref, t)
        h_tile = jnp.concatenate([hn_ref[t:t + 1, p * LANES:(p + 1) * LANES] for p in range(8)], axis=0)
        h_hi, h_lo = _split_bf16(h_tile)
        d = lax.dot_general(jnp.concatenate([h_hi, h_lo], axis=0), rows, (((1,), (1,)), ((), ())),
                            preferred_element_type=F32)
        d = d[0:8] + d[8:16]
        r_scr[pl.ds(t, 1), :] = jnp.sum(jnp.where(own, d, 0.0), axis=0, keepdims=True)

    fold = (lax.broadcasted_iota(jnp.int32, (PEER_SLOTS * 8, PEER_SLOTS), 0) // 8
            == lax.broadcasted_iota(jnp.int32, (PEER_SLOTS * 8, PEER_SLOTS), 1)).astype(BF16)
    r_hi, r_lo = _split_bf16(r_scr[...])
    a = jnp.dot(r_hi, fold, preferred_element_type=F32) + jnp.dot(r_lo, fold, preferred_element_type=F32)
    act = 0.5 * a * (1.0 + lax.erf(a * (2.0 ** -0.5)))
    w_ref[...] = gate_ref[...] * act


def _peer_u(eid, table, hn, gate, tm):
    t, d = hn.shape
    return pl.pallas_call(
        _peer_u_kernel,
        grid=(t // tm,),
        in_specs=[pl.BlockSpec((tm, PEER_SLOTS), lambda i: (i, 0), memory_space=pltpu.SMEM,
                               pipeline_mode=pl.Buffered(1)),
                  pl.BlockSpec(memory_space=pltpu.VMEM),
                  pl.BlockSpec((tm, d), lambda i: (i, 0)),
                  pl.BlockSpec((tm, PEER_SLOTS), lambda i: (i, 0))],
        out_specs=pl.BlockSpec((tm, PEER_SLOTS), lambda i: (i, 0)),
        out_shape=jax.ShapeDtypeStruct((t, PEER_SLOTS), F32),
        scratch_shapes=[pltpu.VMEM((tm, PEER_SLOTS * 8), F32)],
        compiler_params=_cparams(1),
        name="peer_u",
    )(eid, table, hn, gate)


def _peer_v_kernel(eid_ref, w_ref, tab_ref, x_ref, out_ref, whi_scr, wlo_scr):
    tm = x_ref.shape[0]
    own = _slot_masks()
    spread = (lax.broadcasted_iota(jnp.int32, (PEER_SLOTS, PEER_SLOTS * 8), 1) // 8
              == lax.broadcasted_iota(jnp.int32, (PEER_SLOTS, PEER_SLOTS * 8), 0)).astype(BF16)
    w_hi, w_lo = _split_bf16(w_ref[...])
    whi_scr[...] = jnp.dot(w_hi, spread, preferred_element_type=F32)
    wlo_scr[...] = jnp.dot(w_lo, spread, preferred_element_type=F32)

    for t in range(tm):
        rows = _gather_rows(eid_ref, tab_ref, t)
        lhs = jnp.concatenate([jnp.where(own, whi_scr[pl.ds(t, 1), :], 0.0),
                               jnp.where(own, wlo_scr[pl.ds(t, 1), :], 0.0)], axis=0).astype(BF16)
        o = jnp.dot(lhs, rows, preferred_element_type=F32)
        o = o[0:8] + o[8:16]
        for p in range(8):
            cols = slice(p * LANES, (p + 1) * LANES)
            out_ref[t:t + 1, cols] = x_ref[t:t + 1, cols] + o[p:p + 1]


def _peer_v(eid, w, table, x, tm):
    t, d = x.shape
    return pl.pallas_call(
        _peer_v_kernel,
        grid=(t // tm,),
        in_specs=[pl.BlockSpec((tm, PEER_SLOTS), lambda i: (i, 0), memory_space=pltpu.SMEM,
                               pipeline_mode=pl.Buffered(1)),
                  pl.BlockSpec((tm, PEER_SLOTS), lambda i: (i, 0)),
                  pl.BlockSpec(memory_space=pltpu.VMEM),
                  pl.BlockSpec((tm, d), lambda i: (i, 0))],
        out_specs=pl.BlockSpec((tm, d), lambda i: (i, 0)),
        out_shape=jax.ShapeDtypeStruct(x.shape, F32),
        scratch_shapes=[pltpu.VMEM((tm, PEER_SLOTS * 8), F32),
                        pltpu.VMEM((tm, PEER_SLOTS * 8), F32)],
        compiler_params=_cparams(1),
        name="peer_v",
    )(eid, w, table, x)


def _final_norm_kernel(x_ref, g_ref, o_ref):
    o_ref[...] = _rms(x_ref[...], g_ref[...])


def _final_norm(x, g, tm):
    t, d = x.shape
    return pl.pallas_call(
        _final_norm_kernel,
        grid=(t // tm,),
        in_specs=[pl.BlockSpec((tm, d), lambda i: (i, 0)), pl.BlockSpec((1, d), lambda i: (0, 0))],
        out_specs=pl.BlockSpec((tm, d), lambda i: (i, 0)),
        out_shape=jax.ShapeDtypeStruct((t, d), F32),
        compiler_params=_cparams(1),
        name="final_norm",
    )(x, g)


def _rope_angles(pos, dim):
    inv = ROPE_THETA ** (-jnp.arange(0, dim, 2, dtype=F32) / dim)
    ang = pos.astype(F32)[:, None] * inv[None, :]
    return jnp.cos(ang), jnp.sin(ang)


def _rope_table(parts, s_len):
    cs, sn, sp = [], [], []
    for width, cos, sin, half in parts:
        if cos is None:
            cs.append(jnp.ones((s_len, width), F32))
            sn.append(jnp.zeros((s_len, width), F32))
            sp.append(jnp.zeros((s_len, width), F32))
            continue
        reps = width // (2 * half)
        zero = jnp.zeros_like(sin)
        cs.append(jnp.tile(jnp.concatenate([cos, cos], axis=-1), (1, reps)))
        sn.append(jnp.tile(jnp.concatenate([-sin, zero], axis=-1), (1, reps)))
        sp.append(jnp.tile(jnp.concatenate([zero, sin], axis=-1), (1, reps)))
    return jnp.stack([jnp.concatenate(cs, -1), jnp.concatenate(sn, -1), jnp.concatenate(sp, -1)])


def kernel(x, g_mix, w_in, win_sink, ax_q_norm, ax_k_norm, mla_q_norm, mla_kv_norm, mla_w_uq, mla_w_ukv,
           g_group, w_out, g_ffn, peer_w_q, peer_sub_keys, peer_u, peer_v, g_final):
    b, s, d = x.shape
    t = b * s
    depth = w_in.shape[0]
    hd = HEAD_DIM

    pos = jnp.arange(s)
    cos1, sin1 = _rope_angles(pos, hd)
    cos_m, sin_m = _rope_angles(pos, MLA_ROPE)
    cos_r, sin_r = _rope_angles(pos // GRID_W, hd // 2)
    cos_c, sin_c = _rope_angles(pos % GRID_W, hd // 2)
    taba = _rope_table([(LANES, cos1, sin1, hd // 2)], s)
    tabb = _rope_table([(32, cos_r, sin_r, 16), (32, cos_c, sin_c, 16)] * 2, s)
    tabd = _rope_table([(MLA_NOPE, None, None, 0), (MLA_ROPE, cos_m, sin_m, 16), (32, None, None, 0)], s)
    head_mean = jnp.asarray(np.kron(np.eye(LANES // hd), np.full((hd, hd), 1.0 / hd)), F32)

    tm_prep = min(256, s)
    tm_tok = 256
    tm_gather = 256
    xf = x
    for l in range(depth):
        w_l = w_in[l]
        kr_cols = jnp.concatenate([jnp.zeros((d, MLA_NOPE), F32), w_l[:, 1920:1952], jnp.zeros((d, 32), F32)], -1)
        w_pad = jnp.concatenate([w_l[:, :1920], kr_cols], axis=-1).astype(BF16)
        wuq = mla_w_uq[l].reshape(-1, 4, MLA_NOPE + MLA_ROPE)
        wuq = jnp.pad(wuq, ((0, 0), (0, 0), (0, MLA_PAD - MLA_NOPE - MLA_ROPE))).reshape(-1, 4 * MLA_PAD).astype(BF16)
        wukv = mla_w_ukv[l].reshape(-1, 4, MLA_NOPE + hd)
        wuk = jnp.pad(wukv[:, :, :MLA_NOPE], ((0, 0), (0, 0), (0, MLA_PAD - MLA_NOPE))).reshape(-1, 4 * MLA_PAD).astype(BF16)
        wuv = wukv[:, :, MLA_NOPE:].reshape(-1, 4 * hd).astype(BF16)
        gq = jnp.tile(ax_q_norm[l], 2)[None]
        gk = jnp.tile(ax_k_norm[l], 2)[None]

        (qa, ka, va, qb, kb, vb, qc, kc, vc, qc4, kc4, vc4, qc16, kc16, vc16, qd, kd, vd) = _prep(
            xf.reshape(b, s, d), g_mix[l][None], w_pad, taba, tabb, tabd, gq, gk,
            mla_q_norm[l][None], mla_kv_norm[l][None], wuq, wuk, wuv, head_mean, tm_prep)

        oa = _banded(qa, ka, va, WIN_RADIUS, sink=win_sink[l])
        ob = _dense(qb, kb, vb, tm_prep)
        od = _dense(qd, kd, vd, tm_prep)
        dil = [a.reshape(t, 4 * hd) for a in _banded(qc, kc, vc, DIL_RADIUS)]
        for stride, (q_r, k_r, v_r) in zip(DIL_STRIDES[1:], ((qc4, kc4, vc4), (qc16, kc16, vc16))):
            seqs = lambda a: a.reshape((b * stride,) + a.shape[2:])
            o, lse = _banded(seqs(q_r), seqs(k_r), seqs(v_r), DIL_RADIUS)
            dil += [o.reshape(b, stride, s // stride, 4 * hd), lse.reshape(b, stride, s // stride, 4 * hd)]

        xf = _mix_out(xf.reshape(t, d), oa.reshape(t, 4 * hd), ob.reshape(t, 4 * hd), od.reshape(t, 4 * hd),
                      *dil, g_group[l], w_out[l].astype(BF16), tm_tok, b)

        hn, eid, gate = _peer_sel(xf, g_ffn[l][None], peer_w_q[l].astype(BF16),
                                  peer_sub_keys[l].astype(BF16), tm_tok)
        wgt = _peer_u(eid, _pack_table(peer_u, l), hn, gate, tm_gather)
        xf = _peer_v(eid, wgt, _pack_table(peer_v, l), xf, tm_gather)

    return _final_norm(xf, g_final[None], tm_tok).reshape(b, s, d)
```

```python
import functools
import math

import jax
import jax.numpy as jnp
import numpy as np
from jax import lax
from jax.experimental import pallas as pl
from jax.experimental.pallas import tpu as pltpu

F32 = jnp.float32
BF16 = jnp.bfloat16

EPS = 1e-6
NEG = -1e30
ROPE_THETA = 10000.0
HEAD_DIM = 64
GRID_W = 64
WIN_RADIUS = 128
DIL_RADIUS = 64
DIL_STRIDES = (1, 4, 16)
MLA_NOPE = 64
MLA_ROPE = 32
MLA_PAD = 128
PEER_KEYS = 128
PEER_HEADS = 8
PEER_TOPK = 16
PEER_SLOTS = PEER_HEADS * PEER_TOPK
IN_PAD = 2048

LANES = 128
ROW_WORDS = 4
PACK_ROWS = 512
VMEM_LIMIT = 56 * 1024 * 1024


def _cparams(n_axes, vmem=None):
    return pltpu.CompilerParams(
        dimension_semantics=("arbitrary",) * n_axes,
        vmem_limit_bytes=vmem or VMEM_LIMIT)


def _rms(x, g):
    return x * lax.rsqrt(jnp.mean(x * x, axis=-1, keepdims=True) + EPS) * g


def _rope(x, tab_ref, shift):
    n = x.shape[-1]
    return (x * tab_ref[0] + pltpu.roll(x, n - shift, 1) * tab_ref[1]
            + pltpu.roll(x, shift, 1) * tab_ref[2])


def _prep_kernel(x_ref, g_ref, w_ref, taba_ref, tabb_ref, tabd_ref, gq_ref, gk_ref,
                 gcq_ref, gckv_ref, wuq_ref, wuk_ref, wuv_ref, hm_ref,
                 qa, ka, va, qb, kb, vb, qc, kc, vc, qc4, kc4, vc4, qc16, kc16, vc16, qd, kd, vd,
                 z_scr, c_scr):
    h = _rms(x_ref[...], g_ref[...])
    z_scr[...] = jnp.dot(h.astype(BF16), w_ref[...], preferred_element_type=F32)
    hd = HEAD_DIM
    qscale = hd ** -0.5

    def put_heads(dst, first, val):
        for i in range(val.shape[-1] // hd):
            dst[first + i] = val[:, i * hd:(i + 1) * hd].astype(dst.dtype)

    def rotary_group(base, q_dst, k_dst, v_dst):
        for half in range(2):
            q = _rope(z_scr[:, base + half * LANES: base + (half + 1) * LANES], taba_ref, hd // 2)
            put_heads(q_dst, 2 * half, q * qscale)
        put_heads(k_dst, 0, _rope(z_scr[:, base + 256: base + 384], taba_ref, hd // 2))
        put_heads(v_dst, 0, z_scr[:, base + 384: base + 512])

    rotary_group(0, qa, ka, va)

    for half in range(2):
        q = _rope(z_scr[:, 1024 + half * LANES: 1024 + (half + 1) * LANES], taba_ref, hd // 2)
        c_scr[half] = q * qscale
    c_scr[2] = _rope(z_scr[:, 1280:1408], taba_ref, hd // 2)
    c_scr[3] = z_scr[:, 1408:1536]
    rows = c_scr.shape[1]

    def emit(dsts, take):
        q_dst, k_dst, v_dst = dsts
        put_heads(q_dst, 0, take(0))
        put_heads(q_dst, 2, take(1))
        put_heads(k_dst, 0, take(2))
        put_heads(v_dst, 0, take(3))

    emit((qc, kc, vc), lambda c: c_scr[c])
    for stride, dsts in zip(DIL_STRIDES[1:], ((qc4, kc4, vc4), (qc16, kc16, vc16))):
        for p in range(stride):
            emit([dst.at[p] for dst in dsts],
                 lambda c: c_scr[c, pl.ds(p, rows // stride, stride=stride), :])

    def head_norm(x, g):
        ms = jnp.dot(x * x, hm_ref[...], preferred_element_type=F32)
        return x * lax.rsqrt(ms + EPS) * g

    for half in range(2):
        q = head_norm(z_scr[:, 512 + half * LANES: 512 + (half + 1) * LANES], gq_ref[...])
        put_heads(qb, 2 * half, _rope(q, tabb_ref, hd // 4) * qscale)
    k = head_norm(z_scr[:, 768:896], gk_ref[...])
    put_heads(kb, 0, _rope(k, tabb_ref, hd // 4))
    put_heads(vb, 0, z_scr[:, 896:1024])

    cq = _rms(z_scr[:, 1536:1792], gcq_ref[...]).astype(BF16)
    qfull = jnp.dot(cq, wuq_ref[...], preferred_element_type=F32)
    dscale = (MLA_NOPE + MLA_ROPE) ** -0.5
    ckv = _rms(z_scr[:, 1792:1920], gckv_ref[...]).astype(BF16)
    kfull = jnp.dot(ckv, wuk_ref[...], preferred_element_type=F32)
    vfull = jnp.dot(ckv, wuv_ref[...], preferred_element_type=F32)
    kr = _rope(z_scr[:, 1920:2048], tabd_ref, MLA_ROPE // 2)
    for i in range(4):
        sl = slice(i * MLA_PAD, (i + 1) * MLA_PAD)
        qd[i] = (_rope(qfull[:, sl], tabd_ref, MLA_ROPE // 2) * dscale).astype(qd.dtype)
        kd[i] = (kfull[:, sl] + kr).astype(kd.dtype)
    put_heads(vd, 0, vfull)


def _prep(x, g, w_pad, taba, tabb, tabd, gq, gk, gcq, gckv, wuq, wuk, wuv, hm, tm):
    b, s, d = x.shape
    nt = s // tm
    full = lambda shape: pl.BlockSpec(shape, lambda bi, i: (0,) * len(shape))
    tab = pl.BlockSpec((3, tm, LANES), lambda bi, i: (0, i, 0))

    def hspec(nh, w):
        return pl.BlockSpec((None, nh, tm, w), lambda bi, i: (bi, 0, i, 0))

    def hshape(nh, w):
        return jax.ShapeDtypeStruct((b, nh, s, w), BF16)

    trio_specs = [hspec(4, 64), hspec(2, 64), hspec(2, 64)]
    trio_shapes = [hshape(4, 64), hshape(2, 64), hshape(2, 64)]
    strided_specs, strided_shapes = [], []
    for r in DIL_STRIDES[1:]:
        for nh in (4, 2, 2):
            strided_specs.append(pl.BlockSpec((None, r, nh, tm // r, 64), lambda bi, i: (bi, 0, 0, i, 0)))
            strided_shapes.append(jax.ShapeDtypeStruct((b, r, nh, s // r, 64), BF16))
    return pl.pallas_call(
        _prep_kernel,
        grid=(b, nt),
        in_specs=[pl.BlockSpec((None, tm, d), lambda bi, i: (bi, i, 0)),
                  full((1, d)), full((d, IN_PAD)), tab, tab, tab,
                  full((1, LANES)), full((1, LANES)), full((1, 256)), full((1, LANES)),
                  full((256, 4 * MLA_PAD)), full((LANES, 4 * MLA_PAD)), full((LANES, 256)),
                  full((LANES, LANES))],
        out_specs=trio_specs * 3 + strided_specs + [hspec(4, MLA_PAD), hspec(4, MLA_PAD), hspec(4, 64)],
        out_shape=trio_shapes * 3 + strided_shapes + [hshape(4, MLA_PAD), hshape(4, MLA_PAD), hshape(4, 64)],
        scratch_shapes=[pltpu.VMEM((tm, IN_PAD), F32), pltpu.VMEM((4, tm, LANES), F32)],
        compiler_params=_cparams(2),
        name="prep",
    )(x, g, w_pad, taba, tabb, tabd, gq, gk, gcq, gckv, wuq, wuk, wuv, hm)


def _banded_kernel(*refs, radius, tq, win, seq, group, with_sink):
    if with_sink:
        sink_ref, q_ref, k_ref, v_ref, o_ref = refs
    else:
        q_ref, k_ref, v_ref, o_ref, lse_ref = refs
    q0 = pl.program_id(1) * tq
    start = pl.multiple_of(jnp.clip(q0 - radius, 0, seq - win), 64)
    qpos = q0 + lax.broadcasted_iota(jnp.int32, (tq, win), 0)
    kpos = start + lax.broadcasted_iota(jnp.int32, (tq, win), 1)
    valid = jnp.abs(kpos - qpos) <= radius
    hd = HEAD_DIM
    for h in range(q_ref.shape[0]):
        k = k_ref[h // group, pl.ds(start, win), :]
        v = v_ref[h // group, pl.ds(start, win), :]
        s = lax.dot_general(q_ref[h], k, (((1,), (1,)), ((), ())), preferred_element_type=F32)
        s = jnp.where(valid, s, NEG)
        m = jnp.max(s, axis=-1, keepdims=True)
        if with_sink:
            sink = sink_ref[h]
            m = jnp.maximum(m, sink)
        e = jnp.exp(s - m)
        den = jnp.sum(e, axis=-1, keepdims=True)
        if with_sink:
            den = den + jnp.exp(sink - m)
        o = jnp.dot(e.astype(BF16), v, preferred_element_type=F32) / den
        o_ref[:, h * hd:(h + 1) * hd] = o
        if not with_sink:
            lse_ref[:, h * hd:(h + 1) * hd] = jnp.broadcast_to(m + jnp.log(den), (tq, hd))


def _banded(q, k, v, radius, sink=None):
    nb, nh, seq, hd = q.shape
    nkv = k.shape[1]
    tq = min(256, seq)
    win = min(seq, tq + 2 * radius)
    with_sink = sink is not None
    kern = functools.partial(_banded_kernel, radius=radius, tq=tq, win=win, seq=seq,
                             group=nh // nkv, with_sink=with_sink)
    qspec = pl.BlockSpec((None, nh, tq, hd), lambda b, i: (b, 0, i, 0))
    kvspec = pl.BlockSpec((None, nkv, seq, hd), lambda b, i: (b, 0, 0, 0))
    ospec = pl.BlockSpec((None, tq, nh * hd), lambda b, i: (b, i, 0))
    oshape = jax.ShapeDtypeStruct((nb, seq, nh * hd), F32)
    in_specs = [qspec, kvspec, kvspec]
    args = [q, k, v]
    if with_sink:
        in_specs = [pl.BlockSpec(memory_space=pltpu.SMEM)] + in_specs
        args = [sink] + args
    return pl.pallas_call(
        kern,
        grid=(nb, seq // tq),
        in_specs=in_specs,
        out_specs=ospec if with_sink else [ospec, ospec],
        out_shape=oshape if with_sink else [oshape, oshape],
        compiler_params=_cparams(2),
        name="banded_sink" if with_sink else "banded_stats",
    )(*args)


def _dense_kernel(q_ref, k_ref, v_ref, o_ref, *, group):
    hd = v_ref.shape[-1]
    for h in range(q_ref.shape[0]):
        s = lax.dot_general(q_ref[h], k_ref[h // group], (((1,), (1,)), ((), ())),
                            preferred_element_type=F32)
        m = jnp.max(s, axis=-1, keepdims=True)
        e = jnp.exp(s - m)
        den = jnp.sum(e, axis=-1, keepdims=True)
        o = jnp.dot(e.astype(BF16), v_ref[h // group], preferred_element_type=F32)
        o_ref[:, h * hd:(h + 1) * hd] = o / den


def _dense(q, k, v, tq):
    b, nh, s, dk = q.shape
    nkv = k.shape[1]
    hd = v.shape[-1]
    return pl.pallas_call(
        functools.partial(_dense_kernel, group=nh // nkv),
        grid=(b, s // tq),
        in_specs=[pl.BlockSpec((None, nh, tq, dk), lambda bi, i: (bi, 0, i, 0)),
                  pl.BlockSpec((None, nkv, s, dk), lambda bi, i: (bi, 0, 0, 0)),
                  pl.BlockSpec((None, nkv, s, hd), lambda bi, i: (bi, 0, 0, 0))],
        out_specs=pl.BlockSpec((None, tq, nh * hd), lambda bi, i: (bi, i, 0)),
        out_shape=jax.ShapeDtypeStruct((b, s, nh * hd), F32),
        compiler_params=_cparams(2),
        name="dense_attn",
    )(q, k, v)


def _mix_out_kernel(x_ref, oa_ref, ob_ref, od_ref, o1_ref, l1_ref, o4_ref, l4_ref, o16_ref, l16_ref,
                    gg_ref, w_ref, out_ref, t_scr):
    def token_order(ref):
        stride, part = ref.shape[0], ref.shape[1]
        for p in range(stride):
            for c in range(2):
                t_scr[c, pl.ds(p, part, stride=stride), :] = ref[p, :, c * LANES:(c + 1) * LANES]
        return jnp.concatenate([t_scr[0], t_scr[1]], axis=-1)

    l1, l4, l16 = l1_ref[...], token_order(l4_ref), token_order(l16_ref)
    lm = jnp.maximum(jnp.maximum(l1, l4), l16)
    w1, w4, w16 = jnp.exp(l1 - lm), jnp.exp(l4 - lm), jnp.exp(l16 - lm)
    oc = (w1 * o1_ref[...] + w4 * token_order(o4_ref) + w16 * token_order(o16_ref)) / (w1 + w4 + w16)
    gg = gg_ref[...]
    y = jnp.concatenate([_rms(oa_ref[...], gg[0:1]), _rms(ob_ref[...], gg[1:2]),
                         _rms(oc, gg[2:3]), _rms(od_ref[...], gg[3:4])], axis=-1)
    out_ref[...] = x_ref[...] + jnp.dot(y.astype(BF16), w_ref[...], preferred_element_type=F32)


def _mix_out(x, oa, ob, od, o1, l1, o4, l4, o16, l16, gg, w, tm, batch):
    t, d = x.shape
    nt = t // batch // tm
    row = lambda w_: pl.BlockSpec((tm, w_), lambda bi, i: (bi * nt + i, 0))
    full = lambda shape: pl.BlockSpec(shape, lambda bi, i: (0,) * len(shape))
    strided = lambda r: pl.BlockSpec((None, r, tm // r, 256), lambda bi, i: (bi, 0, i, 0))
    return pl.pallas_call(
        _mix_out_kernel,
        grid=(batch, nt),
        in_specs=[row(d)] + [row(256)] * 5 + [strided(4)] * 2 + [strided(16)] * 2 + [full(gg.shape), full(w.shape)],
        out_specs=row(d),
        out_shape=jax.ShapeDtypeStruct((t, d), F32),
        scratch_shapes=[pltpu.VMEM((2, tm, LANES), F32)],
        compiler_params=_cparams(2),
        name="mix_out",
    )(x, oa, ob, od, o1, l1, o4, l4, o16, l16, gg, w)


def _topk_rows(vals, ids, k):
    best, picked = [], []
    for _ in range(k):
        m = jnp.max(vals, axis=0, keepdims=True)
        idx = jnp.min(jnp.where(vals == m, ids, jnp.float32(2 ** 30)), axis=0, keepdims=True)
        best.append(m)
        picked.append(idx)
        vals = jnp.where(ids == idx, -jnp.inf, vals)
    return jnp.concatenate(best, axis=0), jnp.concatenate(picked, axis=0)


_PAIR_LIMIT = tuple(PEER_TOPK // (i + 1) for i in range(PEER_TOPK))


def _select_experts(s1, i1, s2, i2):
    cols = s1.shape[1]
    row8 = lax.broadcasted_iota(jnp.int32, (8, cols), 0)
    row16 = lax.broadcasted_iota(jnp.int32, (PEER_TOPK, cols), 0).astype(F32)
    cand, flat = [s1[0:1] + s2], [row16]
    for i in range(1, 8):
        cand.append(jnp.where(row8 < _PAIR_LIMIT[i], s1[i:i + 1] + s2[0:8], -jnp.inf))
        flat.append((row8 + i * PEER_TOPK).astype(F32))
    cand.append(s1[8:16] + s2[0:1])
    flat.append(((row8 + 8) * PEER_TOPK).astype(F32))
    best, pick = _topk_rows(jnp.concatenate(cand, axis=0), jnp.concatenate(flat, axis=0), PEER_TOPK)
    pick = pick.astype(jnp.int32)
    pi, pj = pick // PEER_TOPK, pick % PEER_TOPK
    k1 = jnp.zeros(pick.shape, F32)
    k2 = jnp.zeros(pick.shape, F32)
    for r in range(PEER_TOPK):
        k1 = jnp.where(pi == r, i1[r:r + 1], k1)
        k2 = jnp.where(pj == r, i2[r:r + 1], k2)
    e = jnp.exp(best - best[0:1])
    gate = e / jnp.sum(e, axis=0, keepdims=True)
    odd_slot = lax.broadcasted_iota(jnp.int32, pick.shape, 0) % 2
    return gate, (k1 * PEER_KEYS + k2 + PACK_ROWS).astype(jnp.int32) * ROW_WORDS - odd_slot * ROW_WORDS


def _peer_sel_kernel(x_ref, g_ref, wq_ref, keys_ref, hn_ref, eid_ref, gate_ref, q_scr):
    hn = _rms(x_ref[...], g_ref[...])
    hn_ref[...] = hn
    q_scr[...] = jnp.dot(hn.astype(BF16), wq_ref[...], preferred_element_type=F32).astype(BF16)
    tm = x_ref.shape[0]
    key_ids = lax.broadcasted_iota(jnp.int32, (PEER_KEYS, LANES), 0).astype(F32)
    eids, gates = [], []
    for h in range(PEER_HEADS):
        sc = []
        for c in range(2):
            col = (2 * h + c) * LANES
            sc.append(lax.dot_general(keys_ref[h, c], q_scr[:, col:col + LANES],
                                      (((1,), (1,)), ((), ())), preferred_element_type=F32))
        eid_h, gate_h = [], []
        for part in range(tm // LANES):
            lanes = slice(part * LANES, (part + 1) * LANES)
            s1, i1 = _topk_rows(sc[0][:, lanes], key_ids, PEER_TOPK)
            s2, i2 = _topk_rows(sc[1][:, lanes], key_ids, PEER_TOPK)
            gate, eid = _select_experts(s1, i1, s2, i2)
            eid_h.append(eid)
            gate_h.append(gate)
        eids.append(jnp.concatenate(eid_h, axis=1))
        gates.append(jnp.concatenate(gate_h, axis=1))
    eid_ref[...] = jnp.concatenate(eids, axis=0).T
    gate_ref[...] = jnp.concatenate(gates, axis=0).T


def _peer_sel(x, g, wq, keys, tm):
    t, d = x.shape
    full = lambda shape: pl.BlockSpec(shape, lambda i: (0,) * len(shape))
    return pl.pallas_call(
        _peer_sel_kernel,
        grid=(t // tm,),
        in_specs=[pl.BlockSpec((tm, d), lambda i: (i, 0)), full((1, d)), full(wq.shape), full(keys.shape)],
        out_specs=[pl.BlockSpec((tm, d), lambda i: (i, 0)),
                   pl.BlockSpec((tm, PEER_SLOTS), lambda i: (i, 0)),
                   pl.BlockSpec((tm, PEER_SLOTS), lambda i: (i, 0))],
        out_shape=[jax.ShapeDtypeStruct((t, d), F32),
                   jax.ShapeDtypeStruct((t, PEER_SLOTS), jnp.int32),
                   jax.ShapeDtypeStruct((t, PEER_SLOTS), F32)],
        scratch_shapes=[pltpu.VMEM((tm, wq.shape[1]), BF16)],
        compiler_params=_cparams(1),
        name="peer_sel",
    )(x, g, wq, keys)


def _pack_kernel(t_ref, o_ref):
    step = pl.program_id(0)
    is_pad = jnp.logical_or(step == 0, step == pl.num_programs(0) - 1)

    @pl.when(is_pad)
    def _():
        o_ref[...] = jnp.zeros(o_ref.shape, o_ref.dtype)

    @pl.when(jnp.logical_not(is_pad))
    def _():
        x = t_ref[...]
        half = x.shape[1] // 2
        lo = pltpu.bitcast(x[:, :half].astype(BF16).astype(F32), jnp.uint32) >> 16
        hi = pltpu.bitcast(x[:, half:].astype(BF16).astype(F32), jnp.uint32) & jnp.uint32(0xFFFF0000)
        o_ref[...] = lo | hi


def _pack_table(tabs, layer):
    _, n, d = tabs.shape
    nb = n // PACK_ROWS
    words = pl.pallas_call(
        _pack_kernel,
        grid=(nb + 2,),
        in_specs=[pl.BlockSpec((None, PACK_ROWS, d), lambda i: (layer, jnp.clip(i - 1, 0, nb - 1), 0))],
        out_specs=pl.BlockSpec((PACK_ROWS, d // 2), lambda i: (i, 0)),
        out_shape=jax.ShapeDtypeStruct((n + 2 * PACK_ROWS, d // 2), jnp.uint32),
        compiler_params=_cparams(1),
        name="pack_table",
    )(tabs)
    return words.reshape((n + 2 * PACK_ROWS) * ROW_WORDS, LANES)


def _slot_masks():
    col = lax.broadcasted_iota(jnp.int32, (8, PEER_SLOTS * 8), 1) % 8
    row = lax.broadcasted_iota(jnp.int32, (8, PEER_SLOTS * 8), 0)
    return col == 2 * (row % ROW_WORDS) + row // ROW_WORDS


def _split_bf16(x):
    hi = x.astype(BF16)
    return hi, (x - hi.astype(F32)).astype(BF16)


def _gather_rows(eid_ref, tab_ref, t):
    low_half = lax.broadcasted_iota(jnp.int32, (8, LANES), 0) < ROW_WORDS
    pairs = []
    for j in range(0, PEER_SLOTS, 2):
        even = pl.multiple_of(eid_ref[t, j], ROW_WORDS)
        odd = pl.multiple_of(eid_ref[t, j + 1], ROW_WORDS)
        pairs.append(jnp.where(low_half, tab_ref[pl.ds(even, 8), :], tab_ref[pl.ds(odd, 8), :]))
    return pltpu.bitcast(jnp.concatenate(pairs, axis=0), BF16)


def _index_copy(eid_hbm, buf, sem, first_token):
    return pltpu.make_async_copy(eid_hbm.at[pl.ds(first_token, buf.shape[0])], buf, sem)


def _for_tokens(tm, eid_hbm, idx_bufs, sems, token):
    step, nsteps = pl.program_id(0), pl.num_programs(0)
    half = tm // 2
    first = step * tm

    @pl.when(step == 0)
    def _():
        _index_copy(eid_hbm, idx_bufs[0], sems.at[0], 0).start()

    _index_copy(eid_hbm, idx_bufs[1], sems.at[1], first + half).start()
    _index_copy(eid_hbm, idx_bufs[0], sems.at[0], first).wait()
    for t in range(half):
        token(t, idx_bufs[0], t)

    @pl.when(step + 1 < nsteps)
    def _():
        _index_copy(eid_hbm, idx_bufs[0], sems.at[0], first + tm).start()

    _index_copy(eid_hbm, idx_bufs[1], sems.at[1], first + half).wait()
    for t in range(half, tm):
        token(t, idx_bufs[1], t - half)


def _index_scratch(tm):
    half = (tm // 2, PEER_SLOTS)
    return [pltpu.SMEM(half, jnp.int32), pltpu.SMEM(half, jnp.int32), pltpu.SemaphoreType.DMA((2,))]


def _peer_u_kernel(eid_hbm, tab_ref, hn_ref, gate_ref, w_ref, r_scr, idx0, idx1, sems):
    tm = hn_ref.shape[0]
    own = _slot_masks()

    def token(t, idx_ref, row):
        rows = _gather_rows(idx_ref, tab_ref, row)
        h_tile = jnp.concatenate([hn_ref[t:t + 1, p * LANES:(p + 1) * LANES] for p in range(8)], axis=0)
        h_hi, h_lo = _split_bf16(h_tile)
        d = lax.dot_general(jnp.concatenate([h_hi, h_lo], axis=0), rows, (((1,), (1,)), ((), ())),
                            preferred_element_type=F32)
        d = d[0:8] + d[8:16]
        r_scr[pl.ds(t, 1), :] = jnp.sum(jnp.where(own, d, 0.0), axis=0, keepdims=True)

    _for_tokens(tm, eid_hbm, (idx0, idx1), sems, token)
    fold = (lax.broadcasted_iota(jnp.int32, (PEER_SLOTS * 8, PEER_SLOTS), 0) // 8
            == lax.broadcasted_iota(jnp.int32, (PEER_SLOTS * 8, PEER_SLOTS), 1)).astype(BF16)
    r_hi, r_lo = _split_bf16(r_scr[...])
    a = jnp.dot(r_hi, fold, preferred_element_type=F32) + jnp.dot(r_lo, fold, preferred_element_type=F32)
    act = 0.5 * a * (1.0 + lax.erf(a * (2.0 ** -0.5)))
    w_ref[...] = gate_ref[...] * act


def _peer_u(eid, table, hn, gate, tm):
    t, d = hn.shape
    return pl.pallas_call(
        _peer_u_kernel,
        grid=(t // tm,),
        in_specs=[pl.BlockSpec(memory_space=pl.ANY),
                  pl.BlockSpec(memory_space=pltpu.VMEM),
                  pl.BlockSpec((tm, d), lambda i: (i, 0)),
                  pl.BlockSpec((tm, PEER_SLOTS), lambda i: (i, 0))],
        out_specs=pl.BlockSpec((tm, PEER_SLOTS), lambda i: (i, 0)),
        out_shape=jax.ShapeDtypeStruct((t, PEER_SLOTS), F32),
        scratch_shapes=[pltpu.VMEM((tm, PEER_SLOTS * 8), F32)] + _index_scratch(tm),
        compiler_params=_cparams(1),
        name="peer_u",
    )(eid, table, hn, gate)


def _peer_v_kernel(eid_hbm, w_ref, tab_ref, x_ref, out_ref, whi_scr, wlo_scr, idx0, idx1, sems):
    tm = x_ref.shape[0]
    own = _slot_masks()
    spread = (lax.broadcasted_iota(jnp.int32, (PEER_SLOTS, PEER_SLOTS * 8), 1) // 8
              == lax.broadcasted_iota(jnp.int32, (PEER_SLOTS, PEER_SLOTS * 8), 0)).astype(BF16)
    w_hi, w_lo = _split_bf16(w_ref[...])
    whi_scr[...] = jnp.dot(w_hi, spread, preferred_element_type=F32)
    wlo_scr[...] = jnp.dot(w_lo, spread, preferred_element_type=F32)

    def token(t, idx_ref, row):
        rows = _gather_rows(idx_ref, tab_ref, row)
        lhs = jnp.concatenate([jnp.where(own, whi_scr[pl.ds(t, 1), :], 0.0),
                               jnp.where(own, wlo_scr[pl.ds(t, 1), :], 0.0)], axis=0).astype(BF16)
        o = jnp.dot(lhs, rows, preferred_element_type=F32)
        o = o[0:8] + o[8:16]
        for p in range(8):
            cols = slice(p * LANES, (p + 1) * LANES)
            out_ref[t:t + 1, cols] = x_ref[t:t + 1, cols] + o[p:p + 1]

    _for_tokens(tm, eid_hbm, (idx0, idx1), sems, token)


def _peer_v(eid, w, table, x, tm):
    t, d = x.shape
    return pl.pallas_call(
        _peer_v_kernel,
        grid=(t // tm,),
        in_specs=[pl.BlockSpec(memory_space=pl.ANY),
                  pl.BlockSpec((tm, PEER_SLOTS), lambda i: (i, 0)),
                  pl.BlockSpec(memory_space=pltpu.VMEM),
                  pl.BlockSpec((tm, d), lambda i: (i, 0))],
        out_specs=pl.BlockSpec((tm, d), lambda i: (i, 0)),
        out_shape=jax.ShapeDtypeStruct(x.shape, F32),
        scratch_shapes=[pltpu.VMEM((tm, PEER_SLOTS * 8), F32),
                        pltpu.VMEM((tm, PEER_SLOTS * 8), F32)] + _index_scratch(tm),
        compiler_params=_cparams(1),
        name="peer_v",
    )(eid, w, table, x)


def _final_norm_kernel(x_ref, g_ref, o_ref):
    o_ref[...] = _rms(x_ref[...], g_ref[...])


def _final_norm(x, g, tm):
    t, d = x.shape
    return pl.pallas_call(
        _final_norm_kernel,
        grid=(t // tm,),
        in_specs=[pl.BlockSpec((tm, d), lambda i: (i, 0)), pl.BlockSpec((1, d), lambda i: (0, 0))],
        out_specs=pl.BlockSpec((tm, d), lambda i: (i, 0)),
        out_shape=jax.ShapeDtypeStruct((t, d), F32),
        compiler_params=_cparams(1),
        name="final_norm",
    )(x, g)


def _rope_angles(pos, dim):
    inv = ROPE_THETA ** (-jnp.arange(0, dim, 2, dtype=F32) / dim)
    ang = pos.astype(F32)[:, None] * inv[None, :]
    return jnp.cos(ang), jnp.sin(ang)


def _rope_table(parts, s_len):
    cs, sn, sp = [], [], []
    for width, cos, sin, half in parts:
        if cos is None:
            cs.append(jnp.ones((s_len, width), F32))
            sn.append(jnp.zeros((s_len, width), F32))
            sp.append(jnp.zeros((s_len, width), F32))
            continue
        reps = width // (2 * half)
        zero = jnp.zeros_like(sin)
        cs.append(jnp.tile(jnp.concatenate([cos, cos], axis=-1), (1, reps)))
        sn.append(jnp.tile(jnp.concatenate([-sin, zero], axis=-1), (1, reps)))
        sp.append(jnp.tile(jnp.concatenate([zero, sin], axis=-1), (1, reps)))
    return jnp.stack([jnp.concatenate(cs, -1), jnp.concatenate(sn, -1), jnp.concatenate(sp, -1)])


def kernel(x, g_mix, w_in, win_sink, ax_q_norm, ax_k_norm, mla_q_norm, mla_kv_norm, mla_w_uq, mla_w_ukv,
           g_group, w_out, g_ffn, peer_w_q, peer_sub_keys, peer_u, peer_v, g_final):
    b, s, d = x.shape
    t = b * s
    depth = w_in.shape[0]
    hd = HEAD_DIM

    pos = jnp.arange(s)
    cos1, sin1 = _rope_angles(pos, hd)
    cos_m, sin_m = _rope_angles(pos, MLA_ROPE)
    cos_r, sin_r = _rope_angles(pos // GRID_W, hd // 2)
    cos_c, sin_c = _rope_angles(pos % GRID_W, hd // 2)
    taba = _rope_table([(LANES, cos1, sin1, hd // 2)], s)
    tabb = _rope_table([(32, cos_r, sin_r, 16), (32, cos_c, sin_c, 16)] * 2, s)
    tabd = _rope_table([(MLA_NOPE, None, None, 0), (MLA_ROPE, cos_m, sin_m, 16), (32, None, None, 0)], s)
    head_mean = jnp.asarray(np.kron(np.eye(LANES // hd), np.full((hd, hd), 1.0 / hd)), F32)

    tm_prep = min(256, s)
    tm_tok = 256
    tm_gather = 256
    xf = x
    for l in range(depth):
        w_l = w_in[l]
        kr_cols = jnp.concatenate([jnp.zeros((d, MLA_NOPE), F32), w_l[:, 1920:1952], jnp.zeros((d, 32), F32)], -1)
        w_pad = jnp.concatenate([w_l[:, :1920], kr_cols], axis=-1).astype(BF16)
        wuq = mla_w_uq[l].reshape(-1, 4, MLA_NOPE + MLA_ROPE)
        wuq = jnp.pad(wuq, ((0, 0), (0, 0), (0, MLA_PAD - MLA_NOPE - MLA_ROPE))).reshape(-1, 4 * MLA_PAD).astype(BF16)
        wukv = mla_w_ukv[l].reshape(-1, 4, MLA_NOPE + hd)
        wuk = jnp.pad(wukv[:, :, :MLA_NOPE], ((0, 0), (0, 0), (0, MLA_PAD - MLA_NOPE))).reshape(-1, 4 * MLA_PAD).astype(BF16)
        wuv = wukv[:, :, MLA_NOPE:].reshape(-1, 4 * hd).astype(BF16)
        gq = jnp.tile(ax_q_norm[l], 2)[None]
        gk = jnp.tile(ax_k_norm[l], 2)[None]

        (qa, ka, va, qb, kb, vb, qc, kc, vc, qc4, kc4, vc4, qc16, kc16, vc16, qd, kd, vd) = _prep(
            xf.reshape(b, s, d), g_mix[l][None], w_pad, taba, tabb, tabd, gq, gk,
            mla_q_norm[l][None], mla_kv_norm[l][None], wuq, wuk, wuv, head_mean, tm_prep)

        oa = _banded(qa, ka, va, WIN_RADIUS, sink=win_sink[l])
        ob = _dense(qb, kb, vb, tm_prep)
        od = _dense(qd, kd, vd, tm_prep)
        dil = [a.reshape(t, 4 * hd) for a in _banded(qc, kc, vc, DIL_RADIUS)]
        for stride, (q_r, k_r, v_r) in zip(DIL_STRIDES[1:], ((qc4, kc4, vc4), (qc16, kc16, vc16))):
            seqs = lambda a: a.reshape((b * stride,) + a.shape[2:])
            o, lse = _banded(seqs(q_r), seqs(k_r), seqs(v_r), DIL_RADIUS)
            dil += [o.reshape(b, stride, s // stride, 4 * hd), lse.reshape(b, stride, s // stride, 4 * hd)]

        xf = _mix_out(xf.reshape(t, d), oa.reshape(t, 4 * hd), ob.reshape(t, 4 * hd), od.reshape(t, 4 * hd),
                      *dil, g_group[l], w_out[l].astype(BF16), tm_tok, b)

        hn, eid, gate = _peer_sel(xf, g_ffn[l][None], peer_w_q[l].astype(BF16),
                                  peer_sub_keys[l].astype(BF16), tm_tok)
        wgt = _peer_u(eid, _pack_table(peer_u, l), hn, gate, tm_gather)
        xf = _peer_v(eid, wgt, _pack_table(peer_v, l), xf, tm_gather)

    return _final_norm(xf, g_final[None], tm_tok).reshape(b, s, d)
```

```python
import functools
import math

import jax
import jax.numpy as jnp
import numpy as np
from jax import lax
from jax.experimental import pallas as pl
from jax.experimental.pallas import tpu as pltpu

F32 = jnp.float32
BF16 = jnp.bfloat16

EPS = 1e-6
NEG = -1e30
ROPE_THETA = 10000.0
HEAD_DIM = 64
GRID_W = 64
WIN_RADIUS = 128
DIL_RADIUS = 64
DIL_STRIDES = (1, 4, 16)
MLA_NOPE = 64
MLA_ROPE = 32
MLA_PAD = 128
PEER_KEYS = 128
PEER_HEADS = 8
PEER_TOPK = 16
PEER_SLOTS = PEER_HEADS * PEER_TOPK
IN_PAD = 2048

LANES = 128
ROW_WORDS = 4
PACK_ROWS = 512
VMEM_LIMIT = 56 * 1024 * 1024


def _cparams(n_axes, vmem=None):
    return pltpu.CompilerParams(
        dimension_semantics=("arbitrary",) * n_axes,
        vmem_limit_bytes=vmem or VMEM_LIMIT)


def _rms(x, g):
    return x * lax.rsqrt(jnp.mean(x * x, axis=-1, keepdims=True) + EPS) * g


def _rope(x, tab_ref, shift):
    n = x.shape[-1]
    return (x * tab_ref[0] + pltpu.roll(x, n - shift, 1) * tab_ref[1]
            + pltpu.roll(x, shift, 1) * tab_ref[2])


def _prep_kernel(x_ref, g_ref, w_ref, taba_ref, tabb_ref, tabd_ref, gq_ref, gk_ref,
                 gcq_ref, gckv_ref, wuq_ref, wuk_ref, wuv_ref, hm_ref,
                 qa, ka, va, qb, kb, vb, qc, kc, vc, qc4, kc4, vc4, qc16, kc16, vc16, qd, kd, vd,
                 z_scr, c_scr):
    h = _rms(x_ref[...], g_ref[...])
    z_scr[...] = jnp.dot(h.astype(BF16), w_ref[...], preferred_element_type=F32)
    hd = HEAD_DIM
    qscale = hd ** -0.5

    def put_heads(dst, first, val):
        for i in range(val.shape[-1] // hd):
            dst[first + i] = val[:, i * hd:(i + 1) * hd].astype(dst.dtype)

    def rotary_group(base, q_dst, k_dst, v_dst):
        for half in range(2):
            q = _rope(z_scr[:, base + half * LANES: base + (half + 1) * LANES], taba_ref, hd // 2)
            put_heads(q_dst, 2 * half, q * qscale)
        put_heads(k_dst, 0, _rope(z_scr[:, base + 256: base + 384], taba_ref, hd // 2))
        put_heads(v_dst, 0, z_scr[:, base + 384: base + 512])

    rotary_group(0, qa, ka, va)

    for half in range(2):
        q = _rope(z_scr[:, 1024 + half * LANES: 1024 + (half + 1) * LANES], taba_ref, hd // 2)
        c_scr[half] = q * qscale
    c_scr[2] = _rope(z_scr[:, 1280:1408], taba_ref, hd // 2)
    c_scr[3] = z_scr[:, 1408:1536]
    rows = c_scr.shape[1]

    def emit(dsts, take):
        q_dst, k_dst, v_dst = dsts
        put_heads(q_dst, 0, take(0))
        put_heads(q_dst, 2, take(1))
        put_heads(k_dst, 0, take(2))
        put_heads(v_dst, 0, take(3))

    emit((qc, kc, vc), lambda c: c_scr[c])
    for stride, dsts in zip(DIL_STRIDES[1:], ((qc4, kc4, vc4), (qc16, kc16, vc16))):
        for p in range(stride):
            emit([dst.at[p] for dst in dsts],
                 lambda c: c_scr[c, pl.ds(p, rows // stride, stride=stride), :])

    def head_norm(x, g):
        ms = jnp.dot(x * x, hm_ref[...], preferred_element_type=F32)
        return x * lax.rsqrt(ms + EPS) * g

    for half in range(2):
        q = head_norm(z_scr[:, 512 + half * LANES: 512 + (half + 1) * LANES], gq_ref[...])
        put_heads(qb, 2 * half, _rope(q, tabb_ref, hd // 4) * qscale)
    k = head_norm(z_scr[:, 768:896], gk_ref[...])
    put_heads(kb, 0, _rope(k, tabb_ref, hd // 4))
    put_heads(vb, 0, z_scr[:, 896:1024])

    cq = _rms(z_scr[:, 1536:1792], gcq_ref[...]).astype(BF16)
    qfull = jnp.dot(cq, wuq_ref[...], preferred_element_type=F32)
    dscale = (MLA_NOPE + MLA_ROPE) ** -0.5
    ckv = _rms(z_scr[:, 1792:1920], gckv_ref[...]).astype(BF16)
    kfull = jnp.dot(ckv, wuk_ref[...], preferred_element_type=F32)
    vfull = jnp.dot(ckv, wuv_ref[...], preferred_element_type=F32)
    kr = _rope(z_scr[:, 1920:2048], tabd_ref, MLA_ROPE // 2)
    for i in range(4):
        sl = slice(i * MLA_PAD, (i + 1) * MLA_PAD)
        qd[i] = (_rope(qfull[:, sl], tabd_ref, MLA_ROPE // 2) * dscale).astype(qd.dtype)
        kd[i] = (kfull[:, sl] + kr).astype(kd.dtype)
    put_heads(vd, 0, vfull)


def _prep(x, g, w_pad, taba, tabb, tabd, gq, gk, gcq, gckv, wuq, wuk, wuv, hm, tm):
    b, s, d = x.shape
    nt = s // tm
    full = lambda shape: pl.BlockSpec(shape, lambda bi, i: (0,) * len(shape))
    tab = pl.BlockSpec((3, tm, LANES), lambda bi, i: (0, i, 0))

    def hspec(nh, w):
        return pl.BlockSpec((None, nh, tm, w), lambda bi, i: (bi, 0, i, 0))

    def hshape(nh, w):
        return jax.ShapeDtypeStruct((b, nh, s, w), BF16)

    trio_specs = [hspec(4, 64), hspec(2, 64), hspec(2, 64)]
    trio_shapes = [hshape(4, 64), hshape(2, 64), hshape(2, 64)]
    strided_specs, strided_shapes = [], []
    for r in DIL_STRIDES[1:]:
        for nh in (4, 2, 2):
            strided_specs.append(pl.BlockSpec((None, r, nh, tm // r, 64), lambda bi, i: (bi, 0, 0, i, 0)))
            strided_shapes.append(jax.ShapeDtypeStruct((b, r, nh, s // r, 64), BF16))
    return pl.pallas_call(
        _prep_kernel,
        grid=(b, nt),
        in_specs=[pl.BlockSpec((None, tm, d), lambda bi, i: (bi, i, 0)),
                  full((1, d)), full((d, IN_PAD)), tab, tab, tab,
                  full((1, LANES)), full((1, LANES)), full((1, 256)), full((1, LANES)),
                  full((256, 4 * MLA_PAD)), full((LANES, 4 * MLA_PAD)), full((LANES, 256)),
                  full((LANES, LANES))],
        out_specs=trio_specs * 3 + strided_specs + [hspec(4, MLA_PAD), hspec(4, MLA_PAD), hspec(4, 64)],
        out_shape=trio_shapes * 3 + strided_shapes + [hshape(4, MLA_PAD), hshape(4, MLA_PAD), hshape(4, 64)],
        scratch_shapes=[pltpu.VMEM((tm, IN_PAD), F32), pltpu.VMEM((4, tm, LANES), F32)],
        compiler_params=_cparams(2),
        name="prep",
    )(x, g, w_pad, taba, tabb, tabd, gq, gk, gcq, gckv, wuq, wuk, wuv, hm)


def _banded_kernel(*refs, radius, tq, win, seq, group, with_sink):
    if with_sink:
        sink_ref, q_ref, k_ref, v_ref, o_ref = refs
    else:
        q_ref, k_ref, v_ref, o_ref, lse_ref = refs
    q0 = pl.program_id(1) * tq
    start = pl.multiple_of(jnp.clip(q0 - radius, 0, seq - win), 64)
    qpos = q0 + lax.broadcasted_iota(jnp.int32, (tq, win), 0)
    kpos = start + lax.broadcasted_iota(jnp.int32, (tq, win), 1)
    valid = jnp.abs(kpos - qpos) <= radius
    hd = HEAD_DIM
    for h in range(q_ref.shape[0]):
        k = k_ref[h // group, pl.ds(start, win), :]
        v = v_ref[h // group, pl.ds(start, win), :]
        s = lax.dot_general(q_ref[h], k, (((1,), (1,)), ((), ())), preferred_element_type=F32)
        s = jnp.where(valid, s, NEG)
        m = jnp.max(s, axis=-1, keepdims=True)
        if with_sink:
            sink = sink_ref[h]
            m = jnp.maximum(m, sink)
        e = jnp.exp(s - m)
        den = jnp.sum(e, axis=-1, keepdims=True)
        if with_sink:
            den = den + jnp.exp(sink - m)
        o = jnp.dot(e.astype(BF16), v, preferred_element_type=F32) / den
        o_ref[:, h * hd:(h + 1) * hd] = o
        if not with_sink:
            lse_ref[:, h * hd:(h + 1) * hd] = jnp.broadcast_to(m + jnp.log(den), (tq, hd))


def _banded(q, k, v, radius, sink=None):
    nb, nh, seq, hd = q.shape
    nkv = k.shape[1]
    tq = min(256, seq)
    win = min(seq, tq + 2 * radius)
    with_sink = sink is not None
    kern = functools.partial(_banded_kernel, radius=radius, tq=tq, win=win, seq=seq,
                             group=nh // nkv, with_sink=with_sink)
    qspec = pl.BlockSpec((None, nh, tq, hd), lambda b, i: (b, 0, i, 0))
    kvspec = pl.BlockSpec((None, nkv, seq, hd), lambda b, i: (b, 0, 0, 0))
    ospec = pl.BlockSpec((None, tq, nh * hd), lambda b, i: (b, i, 0))
    oshape = jax.ShapeDtypeStruct((nb, seq, nh * hd), F32)
    in_specs = [qspec, kvspec, kvspec]
    args = [q, k, v]
    if with_sink:
        in_specs = [pl.BlockSpec(memory_space=pltpu.SMEM)] + in_specs
        args = [sink] + args
    return pl.pallas_call(
        kern,
        grid=(nb, seq // tq),
        in_specs=in_specs,
        out_specs=ospec if with_sink else [ospec, ospec],
        out_shape=oshape if with_sink else [oshape, oshape],
        compiler_params=_cparams(2),
        name="banded_sink" if with_sink else "banded_stats",
    )(*args)


def _dense_kernel(q_ref, k_ref, v_ref, o_ref, *, group):
    hd = v_ref.shape[-1]
    for h in range(q_ref.shape[0]):
        s = lax.dot_general(q_ref[h], k_ref[h // group], (((1,), (1,)), ((), ())),
                            preferred_element_type=F32)
        m = jnp.max(s, axis=-1, keepdims=True)
        e = jnp.exp(s - m)
        den = jnp.sum(e, axis=-1, keepdims=True)
        o = jnp.dot(e.astype(BF16), v_ref[h // group], preferred_element_type=F32)
        o_ref[:, h * hd:(h + 1) * hd] = o / den


def _dense(q, k, v, tq):
    b, nh, s, dk = q.shape
    nkv = k.shape[1]
    hd = v.shape[-1]
    return pl.pallas_call(
        functools.partial(_dense_kernel, group=nh // nkv),
        grid=(b, s // tq),
        in_specs=[pl.BlockSpec((None, nh, tq, dk), lambda bi, i: (bi, 0, i, 0)),
                  pl.BlockSpec((None, nkv, s, dk), lambda bi, i: (bi, 0, 0, 0)),
                  pl.BlockSpec((None, nkv, s, hd), lambda bi, i: (bi, 0, 0, 0))],
        out_specs=pl.BlockSpec((None, tq, nh * hd), lambda bi, i: (bi, i, 0)),
        out_shape=jax.ShapeDtypeStruct((b, s, nh * hd), F32),
        compiler_params=_cparams(2),
        name="dense_attn",
    )(q, k, v)


def _mix_out_kernel(x_ref, oa_ref, ob_ref, od_ref, o1_ref, l1_ref, o4_ref, l4_ref, o16_ref, l16_ref,
                    gg_ref, w_ref, out_ref, t_scr):
    def token_order(ref):
        stride, part = ref.shape[0], ref.shape[1]
        for p in range(stride):
            for c in range(2):
                t_scr[c, pl.ds(p, part, stride=stride), :] = ref[p, :, c * LANES:(c + 1) * LANES]
        return jnp.concatenate([t_scr[0], t_scr[1]], axis=-1)

    l1, l4, l16 = l1_ref[...], token_order(l4_ref), token_order(l16_ref)
    lm = jnp.maximum(jnp.maximum(l1, l4), l16)
    w1, w4, w16 = jnp.exp(l1 - lm), jnp.exp(l4 - lm), jnp.exp(l16 - lm)
    oc = (w1 * o1_ref[...] + w4 * token_order(o4_ref) + w16 * token_order(o16_ref)) / (w1 + w4 + w16)
    gg = gg_ref[...]
    y = jnp.concatenate([_rms(oa_ref[...], gg[0:1]), _rms(ob_ref[...], gg[1:2]),
                         _rms(oc, gg[2:3]), _rms(od_ref[...], gg[3:4])], axis=-1)
    out_ref[...] = x_ref[...] + jnp.dot(y.astype(BF16), w_ref[...], preferred_element_type=F32)


def _mix_out(x, oa, ob, od, o1, l1, o4, l4, o16, l16, gg, w, tm, batch):
    t, d = x.shape
    nt = t // batch // tm
    row = lambda w_: pl.BlockSpec((tm, w_), lambda bi, i: (bi * nt + i, 0))
    full = lambda shape: pl.BlockSpec(shape, lambda bi, i: (0,) * len(shape))
    strided = lambda r: pl.BlockSpec((None, r, tm // r, 256), lambda bi, i: (bi, 0, i, 0))
    return pl.pallas_call(
        _mix_out_kernel,
        grid=(batch, nt),
        in_specs=[row(d)] + [row(256)] * 5 + [strided(4)] * 2 + [strided(16)] * 2 + [full(gg.shape), full(w.shape)],
        out_specs=row(d),
        out_shape=jax.ShapeDtypeStruct((t, d), F32),
        scratch_shapes=[pltpu.VMEM((2, tm, LANES), F32)],
        compiler_params=_cparams(2),
        name="mix_out",
    )(x, oa, ob, od, o1, l1, o4, l4, o16, l16, gg, w)


def _topk_rows(vals, ids, k):
    best, picked = [], []
    for _ in range(k):
        m = jnp.max(vals, axis=0, keepdims=True)
        idx = jnp.min(jnp.where(vals == m, ids, jnp.float32(2 ** 30)), axis=0, keepdims=True)
        best.append(m)
        picked.append(idx)
        vals = jnp.where(ids == idx, -jnp.inf, vals)
    return jnp.concatenate(best, axis=0), jnp.concatenate(picked, axis=0)


_PAIR_LIMIT = tuple(PEER_TOPK // (i + 1) for i in range(PEER_TOPK))


def _select_experts(s1, i1, s2, i2):
    cols = s1.shape[1]
    row8 = lax.broadcasted_iota(jnp.int32, (8, cols), 0)
    row16 = lax.broadcasted_iota(jnp.int32, (PEER_TOPK, cols), 0).astype(F32)
    s2_low = s2[0:8]

    def packed(groups):
        vals = jnp.full((8, cols), -jnp.inf, F32)
        ids = jnp.zeros((8, cols), F32)
        for i, first_row in reversed(groups):
            in_group = (row8 >= first_row) & (row8 < first_row + _PAIR_LIMIT[i])
            shifted = s2_low if first_row == 0 else pltpu.roll(s2_low, first_row, 0)
            vals = jnp.where(in_group, s1[i:i + 1] + shifted, vals)
            ids = jnp.where(in_group, (row8 - first_row + i * PEER_TOPK).astype(F32), ids)
        return vals, ids

    blocks = [(s1[0:1] + s2, row16),
              packed([(1, 0)]),
              packed([(2, 0), (4, 5)]),
              packed([(3, 0), (5, 4), (6, 6)]),
              packed([(7, 0)]),
              (s1[8:16] + s2[0:1], ((row8 + 8) * PEER_TOPK).astype(F32))]
    best, pick = _topk_rows(jnp.concatenate([v for v, _ in blocks], axis=0),
                            jnp.concatenate([f for _, f in blocks], axis=0), PEER_TOPK)
    pick = pick.astype(jnp.int32)
    pi, pj = pick // PEER_TOPK, pick % PEER_TOPK
    k1 = jnp.zeros(pick.shape, F32)
    k2 = jnp.zeros(pick.shape, F32)
    for r in range(PEER_TOPK):
        k1 = jnp.where(pi == r, i1[r:r + 1], k1)
        k2 = jnp.where(pj == r, i2[r:r + 1], k2)
    e = jnp.exp(best - best[0:1])
    gate = e / jnp.sum(e, axis=0, keepdims=True)
    odd_slot = lax.broadcasted_iota(jnp.int32, pick.shape, 0) % 2
    return gate, (k1 * PEER_KEYS + k2 + PACK_ROWS).astype(jnp.int32) * ROW_WORDS - odd_slot * ROW_WORDS


def _peer_sel_kernel(x_ref, g_ref, wq_ref, keys_ref, hn_ref, eid_ref, gate_ref, q_scr):
    hn = _rms(x_ref[...], g_ref[...])
    hn_ref[...] = hn
    q_scr[...] = jnp.dot(hn.astype(BF16), wq_ref[...], preferred_element_type=F32).astype(BF16)
    tm = x_ref.shape[0]
    key_ids = lax.broadcasted_iota(jnp.int32, (PEER_KEYS, LANES), 0).astype(F32)
    eids, gates = [], []
    for h in range(PEER_HEADS):
        sc = []
        for c in range(2):
            col = (2 * h + c) * LANES
            sc.append(lax.dot_general(keys_ref[h, c], q_scr[:, col:col + LANES],
                                      (((1,), (1,)), ((), ())), preferred_element_type=F32))
        eid_h, gate_h = [], []
        for part in range(tm // LANES):
            lanes = slice(part * LANES, (part + 1) * LANES)
            s1, i1 = _topk_rows(sc[0][:, lanes], key_ids, PEER_TOPK)
            s2, i2 = _topk_rows(sc[1][:, lanes], key_ids, PEER_TOPK)
            gate, eid = _select_experts(s1, i1, s2, i2)
            eid_h.append(eid)
            gate_h.append(gate)
        eids.append(jnp.concatenate(eid_h, axis=1))
        gates.append(jnp.concatenate(gate_h, axis=1))
    eid_ref[...] = jnp.concatenate(eids, axis=0).T
    gate_ref[...] = jnp.concatenate(gates, axis=0).T


def _peer_sel(x, g, wq, keys, tm):
    t, d = x.shape
    full = lambda shape: pl.BlockSpec(shape, lambda i: (0,) * len(shape))
    return pl.pallas_call(
        _peer_sel_kernel,
        grid=(t // tm,),
        in_specs=[pl.BlockSpec((tm, d), lambda i: (i, 0)), full((1, d)), full(wq.shape), full(keys.shape)],
        out_specs=[pl.BlockSpec((tm, d), lambda i: (i, 0)),
                   pl.BlockSpec((tm, PEER_SLOTS), lambda i: (i, 0)),
                   pl.BlockSpec((tm, PEER_SLOTS), lambda i: (i, 0))],
        out_shape=[jax.ShapeDtypeStruct((t, d), F32),
                   jax.ShapeDtypeStruct((t, PEER_SLOTS), jnp.int32),
                   jax.ShapeDtypeStruct((t, PEER_SLOTS), F32)],
        scratch_shapes=[pltpu.VMEM((tm, wq.shape[1]), BF16)],
        compiler_params=_cparams(1),
        name="peer_sel",
    )(x, g, wq, keys)


def _pack_kernel(t_ref, o_ref):
    step = pl.program_id(0)
    is_pad = jnp.logical_or(step == 0, step == pl.num_programs(0) - 1)

    @pl.when(is_pad)
    def _():
        o_ref[...] = jnp.zeros(o_ref.shape, o_ref.dtype)

    @pl.when(jnp.logical_not(is_pad))
    def _():
        x = t_ref[...]
        half = x.shape[1] // 2
        lo = pltpu.bitcast(x[:, :half].astype(BF16).astype(F32), jnp.uint32) >> 16
        hi = pltpu.bitcast(x[:, half:].astype(BF16).astype(F32), jnp.uint32) & jnp.uint32(0xFFFF0000)
        o_ref[...] = lo | hi


def _pack_table(tabs, layer):
    _, n, d = tabs.shape
    nb = n // PACK_ROWS
    words = pl.pallas_call(
        _pack_kernel,
        grid=(nb + 2,),
        in_specs=[pl.BlockSpec((None, PACK_ROWS, d), lambda i: (layer, jnp.clip(i - 1, 0, nb - 1), 0))],
        out_specs=pl.BlockSpec((PACK_ROWS, d // 2), lambda i: (i, 0)),
        out_shape=jax.ShapeDtypeStruct((n + 2 * PACK_ROWS, d // 2), jnp.uint32),
        compiler_params=_cparams(1),
        name="pack_table",
    )(tabs)
    return words.reshape((n + 2 * PACK_ROWS) * ROW_WORDS, LANES)


def _slot_masks():
    col = lax.broadcasted_iota(jnp.int32, (8, PEER_SLOTS * 8), 1) % 8
    row = lax.broadcasted_iota(jnp.int32, (8, PEER_SLOTS * 8), 0)
    return col == 2 * (row % ROW_WORDS) + row // ROW_WORDS


def _split_bf16(x):
    hi = x.astype(BF16)
    return hi, (x - hi.astype(F32)).astype(BF16)


def _gather_rows(eid_ref, tab_ref, t):
    low_half = lax.broadcasted_iota(jnp.int32, (8, LANES), 0) < ROW_WORDS
    pairs = []
    for j in range(0, PEER_SLOTS, 2):
        even = pl.multiple_of(eid_ref[t, j], ROW_WORDS)
        odd = pl.multiple_of(eid_ref[t, j + 1], ROW_WORDS)
        pairs.append(jnp.where(low_half, tab_ref[pl.ds(even, 8), :], tab_ref[pl.ds(odd, 8), :]))
    return pltpu.bitcast(jnp.concatenate(pairs, axis=0), BF16)


def _index_copy(eid_hbm, buf, sem, first_token):
    return pltpu.make_async_copy(eid_hbm.at[pl.ds(first_token, buf.shape[0])], buf, sem)


def _for_tokens(tm, eid_hbm, idx_bufs, sems, token):
    step, nsteps = pl.program_id(0), pl.num_programs(0)
    half = tm // 2
    first = step * tm

    @pl.when(step == 0)
    def _():
        _index_copy(eid_hbm, idx_bufs[0], sems.at[0], 0).start()

    _index_copy(eid_hbm, idx_bufs[1], sems.at[1], first + half).start()
    _index_copy(eid_hbm, idx_bufs[0], sems.at[0], first).wait()
    for t in range(half):
        token(t, idx_bufs[0], t)

    @pl.when(step + 1 < nsteps)
    def _():
        _index_copy(eid_hbm, idx_bufs[0], sems.at[0], first + tm).start()

    _index_copy(eid_hbm, idx_bufs[1], sems.at[1], first + half).wait()
    for t in range(half, tm):
        token(t, idx_bufs[1], t - half)


def _index_scratch(tm):
    half = (tm // 2, PEER_SLOTS)
    return [pltpu.SMEM(half, jnp.int32), pltpu.SMEM(half, jnp.int32), pltpu.SemaphoreType.DMA((2,))]


def _peer_u_kernel(eid_hbm, tab_ref, hn_ref, gate_ref, w_ref, r_scr, idx0, idx1, sems):
    tm = hn_ref.shape[0]
    own = _slot_masks()

    def token(t, idx_ref, row):
        rows = _gather_rows(idx_ref, tab_ref, row)
        h_tile = jnp.concatenate([hn_ref[t:t + 1, p * LANES:(p + 1) * LANES] for p in range(8)], axis=0)
        h_hi, h_lo = _split_bf16(h_tile)
        d = lax.dot_general(jnp.concatenate([h_hi, h_lo], axis=0), rows, (((1,), (1,)), ((), ())),
                            preferred_element_type=F32)
        d = d[0:8] + d[8:16]
        r_scr[pl.ds(t, 1), :] = jnp.sum(jnp.where(own, d, 0.0), axis=0, keepdims=True)

    _for_tokens(tm, eid_hbm, (idx0, idx1), sems, token)
    fold = (lax.broadcasted_iota(jnp.int32, (PEER_SLOTS * 8, PEER_SLOTS), 0) // 8
            == lax.broadcasted_iota(jnp.int32, (PEER_SLOTS * 8, PEER_SLOTS), 1)).astype(BF16)
    r_hi, r_lo = _split_bf16(r_scr[...])
    a = jnp.dot(r_hi, fold, preferred_element_type=F32) + jnp.dot(r_lo, fold, preferred_element_type=F32)
    act = 0.5 * a * (1.0 + lax.erf(a * (2.0 ** -0.5)))
    w_ref[...] = gate_ref[...] * act


def _peer_u(eid, table, hn, gate, tm):
    t, d = hn.shape
    return pl.pallas_call(
        _peer_u_kernel,
        grid=(t // tm,),
        in_specs=[pl.BlockSpec(memory_space=pl.ANY),
                  pl.BlockSpec(memory_space=pltpu.VMEM),
                  pl.BlockSpec((tm, d), lambda i: (i, 0)),
                  pl.BlockSpec((tm, PEER_SLOTS), lambda i: (i, 0))],
        out_specs=pl.BlockSpec((tm, PEER_SLOTS), lambda i: (i, 0)),
        out_shape=jax.ShapeDtypeStruct((t, PEER_SLOTS), F32),
        scratch_shapes=[pltpu.VMEM((tm, PEER_SLOTS * 8), F32)] + _index_scratch(tm),
        compiler_params=_cparams(1),
        name="peer_u",
    )(eid, table, hn, gate)


def _peer_v_kernel(eid_hbm, w_ref, tab_ref, x_ref, out_ref, whi_scr, wlo_scr, idx0, idx1, sems):
    tm = x_ref.shape[0]
    own = _slot_masks()
    spread = (lax.broadcasted_iota(jnp.int32, (PEER_SLOTS, PEER_SLOTS * 8), 1) // 8
              == lax.broadcasted_iota(jnp.int32, (PEER_SLOTS, PEER_SLOTS * 8), 0)).astype(BF16)
    w_hi, w_lo = _split_bf16(w_ref[...])
    whi_scr[...] = jnp.dot(w_hi, spread, preferred_element_type=F32)
    wlo_scr[...] = jnp.dot(w_lo, spread, preferred_element_type=F32)

    def token(t, idx_ref, row):
        rows = _gather_rows(idx_ref, tab_ref, row)
        lhs = jnp.concatenate([jnp.where(own, whi_scr[pl.ds(t, 1), :], 0.0),
                               jnp.where(own, wlo_scr[pl.ds(t, 1), :], 0.0)], axis=0).astype(BF16)
        o = jnp.dot(lhs, rows, preferred_element_type=F32)
        o = o[0:8] + o[8:16]
        for p in range(8):
            cols = slice(p * LANES, (p + 1) * LANES)
            out_ref[t:t + 1, cols] = x_ref[t:t + 1, cols] + o[p:p + 1]

    _for_tokens(tm, eid_hbm, (idx0, idx1), sems, token)


def _peer_v(eid, w, table, x, tm):
    t, d = x.shape
    return pl.pallas_call(
        _peer_v_kernel,
        grid=(t // tm,),
        in_specs=[pl.BlockSpec(memory_space=pl.ANY),
                  pl.BlockSpec((tm, PEER_SLOTS), lambda i: (i, 0)),
                  pl.BlockSpec(memory_space=pltpu.VMEM),
                  pl.BlockSpec((tm, d), lambda i: (i, 0))],
        out_specs=pl.BlockSpec((tm, d), lambda i: (i, 0)),
        out_shape=jax.ShapeDtypeStruct(x.shape, F32),
        scratch_shapes=[pltpu.VMEM((tm, PEER_SLOTS * 8), F32),
                        pltpu.VMEM((tm, PEER_SLOTS * 8), F32)] + _index_scratch(tm),
        compiler_params=_cparams(1),
        name="peer_v",
    )(eid, w, table, x)


def _final_norm_kernel(x_ref, g_ref, o_ref):
    o_ref[...] = _rms(x_ref[...], g_ref[...])


def _final_norm(x, g, tm):
    t, d = x.shape
    return pl.pallas_call(
        _final_norm_kernel,
        grid=(t // tm,),
        in_specs=[pl.BlockSpec((tm, d), lambda i: (i, 0)), pl.BlockSpec((1, d), lambda i: (0, 0))],
        out_specs=pl.BlockSpec((tm, d), lambda i: (i, 0)),
        out_shape=jax.ShapeDtypeStruct((t, d), F32),
        compiler_params=_cparams(1),
        name="final_norm",
    )(x, g)


def _rope_angles(pos, dim):
    inv = ROPE_THETA ** (-jnp.arange(0, dim, 2, dtype=F32) / dim)
    ang = pos.astype(F32)[:, None] * inv[None, :]
    return jnp.cos(ang), jnp.sin(ang)


def _rope_table(parts, s_len):
    cs, sn, sp = [], [], []
    for width, cos, sin, half in parts:
        if cos is None:
            cs.append(jnp.ones((s_len, width), F32))
            sn.append(jnp.zeros((s_len, width), F32))
            sp.append(jnp.zeros((s_len, width), F32))
            continue
        reps = width // (2 * half)
        zero = jnp.zeros_like(sin)
        cs.append(jnp.tile(jnp.concatenate([cos, cos], axis=-1), (1, reps)))
        sn.append(jnp.tile(jnp.concatenate([-sin, zero], axis=-1), (1, reps)))
        sp.append(jnp.tile(jnp.concatenate([zero, sin], axis=-1), (1, reps)))
    return jnp.stack([jnp.concatenate(cs, -1), jnp.concatenate(sn, -1), jnp.concatenate(sp, -1)])


def kernel(x, g_mix, w_in, win_sink, ax_q_norm, ax_k_norm, mla_q_norm, mla_kv_norm, mla_w_uq, mla_w_ukv,
           g_group, w_out, g_ffn, peer_w_q, peer_sub_keys, peer_u, peer_v, g_final):
    b, s, d = x.shape
    t = b * s
    depth = w_in.shape[0]
    hd = HEAD_DIM

    pos = jnp.arange(s)
    cos1, sin1 = _rope_angles(pos, hd)
    cos_m, sin_m = _rope_angles(pos, MLA_ROPE)
    cos_r, sin_r = _rope_angles(pos // GRID_W, hd // 2)
    cos_c, sin_c = _rope_angles(pos % GRID_W, hd // 2)
    taba = _rope_table([(LANES, cos1, sin1, hd // 2)], s)
    tabb = _rope_table([(32, cos_r, sin_r, 16), (32, cos_c, sin_c, 16)] * 2, s)
    tabd = _rope_table([(MLA_NOPE, None, None, 0), (MLA_ROPE, cos_m, sin_m, 16), (32, None, None, 0)], s)
    head_mean = jnp.asarray(np.kron(np.eye(LANES // hd), np.full((hd, hd), 1.0 / hd)), F32)

    tm_prep = min(256, s)
    tm_tok = 256
    tm_gather = 256
    xf = x
    for l in range(depth):
        w_l = w_in[l]
        kr_cols = jnp.concatenate([jnp.zeros((d, MLA_NOPE), F32), w_l[:, 1920:1952], jnp.zeros((d, 32), F32)], -1)
        w_pad = jnp.concatenate([w_l[:, :1920], kr_cols], axis=-1).astype(BF16)
        wuq = mla_w_uq[l].reshape(-1, 4, MLA_NOPE + MLA_ROPE)
        wuq = jnp.pad(wuq, ((0, 0), (0, 0), (0, MLA_PAD - MLA_NOPE - MLA_ROPE))).reshape(-1, 4 * MLA_PAD).astype(BF16)
        wukv = mla_w_ukv[l].reshape(-1, 4, MLA_NOPE + hd)
        wuk = jnp.pad(wukv[:, :, :MLA_NOPE], ((0, 0), (0, 0), (0, MLA_PAD - MLA_NOPE))).reshape(-1, 4 * MLA_PAD).astype(BF16)
        wuv = wukv[:, :, MLA_NOPE:].reshape(-1, 4 * hd).astype(BF16)
        gq = jnp.tile(ax_q_norm[l], 2)[None]
        gk = jnp.tile(ax_k_norm[l], 2)[None]

        (qa, ka, va, qb, kb, vb, qc, kc, vc, qc4, kc4, vc4, qc16, kc16, vc16, qd, kd, vd) = _prep(
            xf.reshape(b, s, d), g_mix[l][None], w_pad, taba, tabb, tabd, gq, gk,
            mla_q_norm[l][None], mla_kv_norm[l][None], wuq, wuk, wuv, head_mean, tm_prep)

        oa = _banded(qa, ka, va, WIN_RADIUS, sink=win_sink[l])
        ob = _dense(qb, kb, vb, min(512, s))
        od = _dense(qd, kd, vd, tm_prep)
        dil = [a.reshape(t, 4 * hd) for a in _banded(qc, kc, vc, DIL_RADIUS)]
        for stride, (q_r, k_r, v_r) in zip(DIL_STRIDES[1:], ((qc4, kc4, vc4), (qc16, kc16, vc16))):
            seqs = lambda a: a.reshape((b * stride,) + a.shape[2:])
            o, lse = _banded(seqs(q_r), seqs(k_r), seqs(v_r), DIL_RADIUS)
            dil += [o.reshape(b, stride, s // stride, 4 * hd), lse.reshape(b, stride, s // stride, 4 * hd)]

        xf = _mix_out(xf.reshape(t, d), oa.reshape(t, 4 * hd), ob.reshape(t, 4 * hd), od.reshape(t, 4 * hd),
                      *dil, g_group[l], w_out[l].astype(BF16), tm_tok, b)

        hn, eid, gate = _peer_sel(xf, g_ffn[l][None], peer_w_q[l].astype(BF16),
                                  peer_sub_keys[l].astype(BF16), tm_tok)
        wgt = _peer_u(eid, _pack_table(peer_u, l), hn, gate, tm_gather)
        xf = _peer_v(eid, wgt, _pack_table(peer_v, l), xf, tm_gather)

    return _final_norm(xf, g_final[None], tm_tok).reshape(b, s, d)
```

```python
import functools
import math

import jax
import jax.numpy as jnp
import numpy as np
from jax import lax
from jax.experimental import pallas as pl
from jax.experimental.pallas import tpu as pltpu

F32 = jnp.float32
BF16 = jnp.bfloat16

EPS = 1e-6
NEG = -1e30
ROPE_THETA = 10000.0
HEAD_DIM = 64
GRID_W = 64
WIN_RADIUS = 128
DIL_RADIUS = 64
DIL_STRIDES = (1, 4, 16)
MLA_NOPE = 64
MLA_ROPE = 32
MLA_PAD = 128
PEER_KEYS = 128
PEER_HEADS = 8
PEER_TOPK = 16
PEER_SLOTS = PEER_HEADS * PEER_TOPK
IN_PAD = 2048

LANES = 128
ROW_WORDS = 4
PACK_ROWS = 512
VMEM_LIMIT = 56 * 1024 * 1024


def _cparams(n_axes, vmem=None):
    return pltpu.CompilerParams(
        dimension_semantics=("arbitrary",) * n_axes,
        vmem_limit_bytes=vmem or VMEM_LIMIT)


def _rms(x, g):
    return x * lax.rsqrt(jnp.mean(x * x, axis=-1, keepdims=True) + EPS) * g


def _rope(x, tab_ref, shift):
    n = x.shape[-1]
    return (x * tab_ref[0] + pltpu.roll(x, n - shift, 1) * tab_ref[1]
            + pltpu.roll(x, shift, 1) * tab_ref[2])


def _prep_kernel(x_ref, g_ref, w_ref, taba_ref, tabb_ref, tabd_ref, gq_ref, gk_ref,
                 gcq_ref, gckv_ref, wuq_ref, wuk_ref, wuv_ref, hm_ref,
                 qa, ka, va, qb, kb, vb, qc, kc, vc, qc4, kc4, vc4, qc16, kc16, vc16, qd, kd, vd,
                 z_scr, c_scr):
    h = _rms(x_ref[...], g_ref[...])
    z_scr[...] = jnp.dot(h.astype(BF16), w_ref[...], preferred_element_type=F32)
    hd = HEAD_DIM
    qscale = hd ** -0.5

    def put_heads(dst, first, val):
        for i in range(val.shape[-1] // hd):
            dst[first + i] = val[:, i * hd:(i + 1) * hd].astype(dst.dtype)

    def rotary_group(base, q_dst, k_dst, v_dst):
        for half in range(2):
            q = _rope(z_scr[:, base + half * LANES: base + (half + 1) * LANES], taba_ref, hd // 2)
            put_heads(q_dst, 2 * half, q * qscale)
        put_heads(k_dst, 0, _rope(z_scr[:, base + 256: base + 384], taba_ref, hd // 2))
        put_heads(v_dst, 0, z_scr[:, base + 384: base + 512])

    rotary_group(0, qa, ka, va)

    for half in range(2):
        q = _rope(z_scr[:, 1024 + half * LANES: 1024 + (half + 1) * LANES], taba_ref, hd // 2)
        c_scr[half] = q * qscale
    c_scr[2] = _rope(z_scr[:, 1280:1408], taba_ref, hd // 2)
    c_scr[3] = z_scr[:, 1408:1536]
    rows = c_scr.shape[1]

    def emit(dsts, take):
        q_dst, k_dst, v_dst = dsts
        put_heads(q_dst, 0, take(0))
        put_heads(q_dst, 2, take(1))
        put_heads(k_dst, 0, take(2))
        put_heads(v_dst, 0, take(3))

    emit((qc, kc, vc), lambda c: c_scr[c])
    for stride, dsts in zip(DIL_STRIDES[1:], ((qc4, kc4, vc4), (qc16, kc16, vc16))):
        for p in range(stride):
            emit([dst.at[p] for dst in dsts],
                 lambda c: c_scr[c, pl.ds(p, rows // stride, stride=stride), :])

    def head_norm(x, g):
        ms = jnp.dot(x * x, hm_ref[...], preferred_element_type=F32)
        return x * lax.rsqrt(ms + EPS) * g

    for half in range(2):
        q = head_norm(z_scr[:, 512 + half * LANES: 512 + (half + 1) * LANES], gq_ref[...])
        put_heads(qb, 2 * half, _rope(q, tabb_ref, hd // 4) * qscale)
    k = head_norm(z_scr[:, 768:896], gk_ref[...])
    put_heads(kb, 0, _rope(k, tabb_ref, hd // 4))
    put_heads(vb, 0, z_scr[:, 896:1024])

    cq = _rms(z_scr[:, 1536:1792], gcq_ref[...]).astype(BF16)
    qfull = jnp.dot(cq, wuq_ref[...], preferred_element_type=F32)
    dscale = (MLA_NOPE + MLA_ROPE) ** -0.5
    ckv = _rms(z_scr[:, 1792:1920], gckv_ref[...]).astype(BF16)
    kfull = jnp.dot(ckv, wuk_ref[...], preferred_element_type=F32)
    vfull = jnp.dot(ckv, wuv_ref[...], preferred_element_type=F32)
    kr = _rope(z_scr[:, 1920:2048], tabd_ref, MLA_ROPE // 2)
    for i in range(4):
        sl = slice(i * MLA_PAD, (i + 1) * MLA_PAD)
        qd[i] = (_rope(qfull[:, sl], tabd_ref, MLA_ROPE // 2) * dscale).astype(qd.dtype)
        kd[i] = (kfull[:, sl] + kr).astype(kd.dtype)
    put_heads(vd, 0, vfull)


def _prep(x, g, w_pad, taba, tabb, tabd, gq, gk, gcq, gckv, wuq, wuk, wuv, hm, tm):
    b, s, d = x.shape
    nt = s // tm
    full = lambda shape: pl.BlockSpec(shape, lambda bi, i: (0,) * len(shape))
    tab = pl.BlockSpec((3, tm, LANES), lambda bi, i: (0, i, 0))

    def hspec(nh, w):
        return pl.BlockSpec((None, nh, tm, w), lambda bi, i: (bi, 0, i, 0))

    def hshape(nh, w):
        return jax.ShapeDtypeStruct((b, nh, s, w), BF16)

    trio_specs = [hspec(4, 64), hspec(2, 64), hspec(2, 64)]
    trio_shapes = [hshape(4, 64), hshape(2, 64), hshape(2, 64)]
    strided_specs, strided_shapes = [], []
    for r in DIL_STRIDES[1:]:
        for nh in (4, 2, 2):
            strided_specs.append(pl.BlockSpec((None, r, nh, tm // r, 64), lambda bi, i: (bi, 0, 0, i, 0)))
            strided_shapes.append(jax.ShapeDtypeStruct((b, r, nh, s // r, 64), BF16))
    return pl.pallas_call(
        _prep_kernel,
        grid=(b, nt),
        in_specs=[pl.BlockSpec((None, tm, d), lambda bi, i: (bi, i, 0)),
                  full((1, d)), full((d, IN_PAD)), tab, tab, tab,
                  full((1, LANES)), full((1, LANES)), full((1, 256)), full((1, LANES)),
                  full((256, 4 * MLA_PAD)), full((LANES, 4 * MLA_PAD)), full((LANES, 256)),
                  full((LANES, LANES))],
        out_specs=trio_specs * 3 + strided_specs + [hspec(4, MLA_PAD), hspec(4, MLA_PAD), hspec(4, 64)],
        out_shape=trio_shapes * 3 + strided_shapes + [hshape(4, MLA_PAD), hshape(4, MLA_PAD), hshape(4, 64)],
        scratch_shapes=[pltpu.VMEM((tm, IN_PAD), F32), pltpu.VMEM((4, tm, LANES), F32)],
        compiler_params=_cparams(2),
        name="prep",
    )(x, g, w_pad, taba, tabb, tabd, gq, gk, gcq, gckv, wuq, wuk, wuv, hm)


def _banded_kernel(*refs, radius, tq, win, seq, group, with_sink):
    if with_sink:
        sink_ref, q_ref, k_ref, v_ref, o_ref = refs
    else:
        q_ref, k_ref, v_ref, o_ref, lse_ref = refs
    q0 = pl.program_id(1) * tq
    start = pl.multiple_of(jnp.clip(q0 - radius, 0, seq - win), 64)
    qpos = q0 + lax.broadcasted_iota(jnp.int32, (tq, win), 0)
    kpos = start + lax.broadcasted_iota(jnp.int32, (tq, win), 1)
    valid = jnp.abs(kpos - qpos) <= radius
    hd = HEAD_DIM
    for h in range(q_ref.shape[0]):
        k = k_ref[h // group, pl.ds(start, win), :]
        v = v_ref[h // group, pl.ds(start, win), :]
        s = lax.dot_general(q_ref[h], k, (((1,), (1,)), ((), ())), preferred_element_type=F32)
        s = jnp.where(valid, s, NEG)
        m = jnp.max(s, axis=-1, keepdims=True)
        if with_sink:
            sink = sink_ref[h]
            m = jnp.maximum(m, sink)
        e = jnp.exp(s - m)
        den = jnp.sum(e, axis=-1, keepdims=True)
        if with_sink:
            den = den + jnp.exp(sink - m)
        o = jnp.dot(e.astype(BF16), v, preferred_element_type=F32) / den
        o_ref[:, h * hd:(h + 1) * hd] = o
        if not with_sink:
            lse_ref[:, h * hd:(h + 1) * hd] = jnp.broadcast_to(m + jnp.log(den), (tq, hd))


def _banded(q, k, v, radius, sink=None):
    nb, nh, seq, hd = q.shape
    nkv = k.shape[1]
    tq = min(256, seq)
    win = min(seq, tq + 2 * radius)
    with_sink = sink is not None
    kern = functools.partial(_banded_kernel, radius=radius, tq=tq, win=win, seq=seq,
                             group=nh // nkv, with_sink=with_sink)
    qspec = pl.BlockSpec((None, nh, tq, hd), lambda b, i: (b, 0, i, 0))
    kvspec = pl.BlockSpec((None, nkv, seq, hd), lambda b, i: (b, 0, 0, 0))
    ospec = pl.BlockSpec((None, tq, nh * hd), lambda b, i: (b, i, 0))
    oshape = jax.ShapeDtypeStruct((nb, seq, nh * hd), F32)
    in_specs = [qspec, kvspec, kvspec]
    args = [q, k, v]
    if with_sink:
        in_specs = [pl.BlockSpec(memory_space=pltpu.SMEM)] + in_specs
        args = [sink] + args
    return pl.pallas_call(
        kern,
        grid=(nb, seq // tq),
        in_specs=in_specs,
        out_specs=ospec if with_sink else [ospec, ospec],
        out_shape=oshape if with_sink else [oshape, oshape],
        compiler_params=_cparams(2),
        name="banded_sink" if with_sink else "banded_stats",
    )(*args)


def _dense_kernel(q_ref, k_ref, v_ref, o_ref, *, group):
    hd = v_ref.shape[-1]
    for h in range(q_ref.shape[0]):
        s = lax.dot_general(q_ref[h], k_ref[h // group], (((1,), (1,)), ((), ())),
                            preferred_element_type=F32)
        m = jnp.max(s, axis=-1, keepdims=True)
        e = jnp.exp(s - m)
        den = jnp.sum(e, axis=-1, keepdims=True)
        o = jnp.dot(e.astype(BF16), v_ref[h // group], preferred_element_type=F32)
        o_ref[:, h * hd:(h + 1) * hd] = o / den


def _dense(q, k, v, tq):
    b, nh, s, dk = q.shape
    nkv = k.shape[1]
    hd = v.shape[-1]
    return pl.pallas_call(
        functools.partial(_dense_kernel, group=nh // nkv),
        grid=(b, s // tq),
        in_specs=[pl.BlockSpec((None, nh, tq, dk), lambda bi, i: (bi, 0, i, 0)),
                  pl.BlockSpec((None, nkv, s, dk), lambda bi, i: (bi, 0, 0, 0)),
                  pl.BlockSpec((None, nkv, s, hd), lambda bi, i: (bi, 0, 0, 0))],
        out_specs=pl.BlockSpec((None, tq, nh * hd), lambda bi, i: (bi, i, 0)),
        out_shape=jax.ShapeDtypeStruct((b, s, nh * hd), F32),
        compiler_params=_cparams(2),
        name="dense_attn",
    )(q, k, v)


def _mix_out_kernel(x_ref, oa_ref, ob_ref, od_ref, o1_ref, l1_ref, o4_ref, l4_ref, o16_ref, l16_ref,
                    gg_ref, w_ref, out_ref, t_scr):
    def token_order(ref):
        stride, part = ref.shape[0], ref.shape[1]
        for p in range(stride):
            for c in range(2):
                t_scr[c, pl.ds(p, part, stride=stride), :] = ref[p, :, c * LANES:(c + 1) * LANES]
        return jnp.concatenate([t_scr[0], t_scr[1]], axis=-1)

    l1, l4, l16 = l1_ref[...], token_order(l4_ref), token_order(l16_ref)
    lm = jnp.maximum(jnp.maximum(l1, l4), l16)
    w1, w4, w16 = jnp.exp(l1 - lm), jnp.exp(l4 - lm), jnp.exp(l16 - lm)
    oc = (w1 * o1_ref[...] + w4 * token_order(o4_ref) + w16 * token_order(o16_ref)) / (w1 + w4 + w16)
    gg = gg_ref[...]
    y = jnp.concatenate([_rms(oa_ref[...], gg[0:1]), _rms(ob_ref[...], gg[1:2]),
                         _rms(oc, gg[2:3]), _rms(od_ref[...], gg[3:4])], axis=-1)
    out_ref[...] = x_ref[...] + jnp.dot(y.astype(BF16), w_ref[...], preferred_element_type=F32)


def _mix_out(x, oa, ob, od, o1, l1, o4, l4, o16, l16, gg, w, tm, batch):
    t, d = x.shape
    nt = t // batch // tm
    row = lambda w_: pl.BlockSpec((tm, w_), lambda bi, i: (bi * nt + i, 0))
    full = lambda shape: pl.BlockSpec(shape, lambda bi, i: (0,) * len(shape))
    strided = lambda r: pl.BlockSpec((None, r, tm // r, 256), lambda bi, i: (bi, 0, i, 0))
    return pl.pallas_call(
        _mix_out_kernel,
        grid=(batch, nt),
        in_specs=[row(d)] + [row(256)] * 5 + [strided(4)] * 2 + [strided(16)] * 2 + [full(gg.shape), full(w.shape)],
        out_specs=row(d),
        out_shape=jax.ShapeDtypeStruct((t, d), F32),
        scratch_shapes=[pltpu.VMEM((2, tm, LANES), F32)],
        compiler_params=_cparams(2),
        name="mix_out",
    )(x, oa, ob, od, o1, l1, o4, l4, o16, l16, gg, w)


def _topk_rows(vals, ids, k):
    best, picked = [], []
    for _ in range(k):
        m = jnp.max(vals, axis=0, keepdims=True)
        idx = jnp.min(jnp.where(vals == m, ids, jnp.float32(2 ** 30)), axis=0, keepdims=True)
        best.append(m)
        picked.append(idx)
        vals = jnp.where(ids == idx, -jnp.inf, vals)
    return jnp.concatenate(best, axis=0), jnp.concatenate(picked, axis=0)


def _sort16_pairs():
    n, pairs, p = 16, [], 1
    while p < n:
        k = p
        while k >= 1:
            for j in range(k % p, n - k, 2 * k):
                for i in range(min(k, n - j - k)):
                    if (i + j) // (2 * p) == (i + j + k) // (2 * p):
                        pairs.append((i + j, i + j + k))
            k //= 2
        p *= 2
    return tuple(pairs)


_SORT16 = _sort16_pairs()


def _topk_keys(scores):
    groups = PEER_KEYS // 8
    sub = lax.broadcasted_iota(jnp.int32, (8, scores.shape[1]), 0).astype(F32)
    vals = [scores[8 * g:8 * (g + 1)] for g in range(groups)]
    ids = [sub + 8.0 * g for g in range(groups)]
    for a, b in _SORT16:
        swap = (vals[b] > vals[a]) | ((vals[b] == vals[a]) & (ids[b] < ids[a]))
        vals[a], vals[b] = jnp.maximum(vals[a], vals[b]), jnp.minimum(vals[a], vals[b])
        ids[a], ids[b] = jnp.where(swap, ids[b], ids[a]), jnp.where(swap, ids[a], ids[b])
    best, picked = [], []
    for it in range(PEER_TOPK):
        m = jnp.max(vals[0], axis=0, keepdims=True)
        idx = jnp.min(jnp.where(vals[0] == m, ids[0], jnp.float32(2 ** 30)), axis=0, keepdims=True)
        best.append(m)
        picked.append(idx)
        popped = ids[0] == idx
        for r in range(PEER_TOPK - 1 - it):
            vals[r] = jnp.where(popped, vals[r + 1], vals[r])
            ids[r] = jnp.where(popped, ids[r + 1], ids[r])
    return jnp.concatenate(best, axis=0), jnp.concatenate(picked, axis=0)


_PAIR_LIMIT = tuple(PEER_TOPK // (i + 1) for i in range(PEER_TOPK))


def _select_experts(s1, i1, s2, i2):
    cols = s1.shape[1]
    row8 = lax.broadcasted_iota(jnp.int32, (8, cols), 0)
    row16 = lax.broadcasted_iota(jnp.int32, (PEER_TOPK, cols), 0).astype(F32)
    s2_low = s2[0:8]

    def packed(groups):
        vals = jnp.full((8, cols), -jnp.inf, F32)
        ids = jnp.zeros((8, cols), F32)
        for i, first_row in reversed(groups):
            in_group = (row8 >= first_row) & (row8 < first_row + _PAIR_LIMIT[i])
            shifted = s2_low if first_row == 0 else pltpu.roll(s2_low, first_row, 0)
            vals = jnp.where(in_group, s1[i:i + 1] + shifted, vals)
            ids = jnp.where(in_group, (row8 - first_row + i * PEER_TOPK).astype(F32), ids)
        return vals, ids

    blocks = [(s1[0:1] + s2, row16),
              packed([(1, 0)]),
              packed([(2, 0), (4, 5)]),
              packed([(3, 0), (5, 4), (6, 6)]),
              packed([(7, 0)]),
              (s1[8:16] + s2[0:1], ((row8 + 8) * PEER_TOPK).astype(F32))]
    best, pick = _topk_rows(jnp.concatenate([v for v, _ in blocks], axis=0),
                            jnp.concatenate([f for _, f in blocks], axis=0), PEER_TOPK)
    pick = pick.astype(jnp.int32)
    pi, pj = pick // PEER_TOPK, pick % PEER_TOPK
    k1 = jnp.zeros(pick.shape, F32)
    k2 = jnp.zeros(pick.shape, F32)
    for r in range(PEER_TOPK):
        k1 = jnp.where(pi == r, i1[r:r + 1], k1)
        k2 = jnp.where(pj == r, i2[r:r + 1], k2)
    e = jnp.exp(best - best[0:1])
    gate = e / jnp.sum(e, axis=0, keepdims=True)
    odd_slot = lax.broadcasted_iota(jnp.int32, pick.shape, 0) % 2
    return gate, (k1 * PEER_KEYS + k2 + PACK_ROWS).astype(jnp.int32) * ROW_WORDS - odd_slot * ROW_WORDS


def _peer_sel_kernel(x_ref, g_ref, wq_ref, keys_ref, hn_ref, eid_ref, gate_ref, q_scr):
    hn = _rms(x_ref[...], g_ref[...])
    hn_ref[...] = hn
    q_scr[...] = jnp.dot(hn.astype(BF16), wq_ref[...], preferred_element_type=F32).astype(BF16)
    tm = x_ref.shape[0]
    eids, gates = [], []
    for h in range(PEER_HEADS):
        sc = []
        for c in range(2):
            col = (2 * h + c) * LANES
            sc.append(lax.dot_general(keys_ref[h, c], q_scr[:, col:col + LANES],
                                      (((1,), (1,)), ((), ())), preferred_element_type=F32))
        eid_h, gate_h = [], []
        for part in range(tm // LANES):
            lanes = slice(part * LANES, (part + 1) * LANES)
            s1, i1 = _topk_keys(sc[0][:, lanes])
            s2, i2 = _topk_keys(sc[1][:, lanes])
            gate, eid = _select_experts(s1, i1, s2, i2)
            eid_h.append(eid)
            gate_h.append(gate)
        eids.append(jnp.concatenate(eid_h, axis=1))
        gates.append(jnp.concatenate(gate_h, axis=1))
    eid_ref[...] = jnp.concatenate(eids, axis=0).T
    gate_ref[...] = jnp.concatenate(gates, axis=0).T


def _peer_sel(x, g, wq, keys, tm):
    t, d = x.shape
    full = lambda shape: pl.BlockSpec(shape, lambda i: (0,) * len(shape))
    return pl.pallas_call(
        _peer_sel_kernel,
        grid=(t // tm,),
        in_specs=[pl.BlockSpec((tm, d), lambda i: (i, 0)), full((1, d)), full(wq.shape), full(keys.shape)],
        out_specs=[pl.BlockSpec((tm, d), lambda i: (i, 0)),
                   pl.BlockSpec((tm, PEER_SLOTS), lambda i: (i, 0)),
                   pl.BlockSpec((tm, PEER_SLOTS), lambda i: (i, 0))],
        out_shape=[jax.ShapeDtypeStruct((t, d), F32),
                   jax.ShapeDtypeStruct((t, PEER_SLOTS), jnp.int32),
                   jax.ShapeDtypeStruct((t, PEER_SLOTS), F32)],
        scratch_shapes=[pltpu.VMEM((tm, wq.shape[1]), BF16)],
        compiler_params=_cparams(1),
        name="peer_sel",
    )(x, g, wq, keys)


def _pack_kernel(t_ref, o_ref):
    step = pl.program_id(0)
    is_pad = jnp.logical_or(step == 0, step == pl.num_programs(0) - 1)

    @pl.when(is_pad)
    def _():
        o_ref[...] = jnp.zeros(o_ref.shape, o_ref.dtype)

    @pl.when(jnp.logical_not(is_pad))
    def _():
        x = t_ref[...]
        half = x.shape[1] // 2
        lo = pltpu.bitcast(x[:, :half].astype(BF16).astype(F32), jnp.uint32) >> 16
        hi = pltpu.bitcast(x[:, half:].astype(BF16).astype(F32), jnp.uint32) & jnp.uint32(0xFFFF0000)
        o_ref[...] = lo | hi


def _pack_table(tabs, layer):
    _, n, d = tabs.shape
    nb = n // PACK_ROWS
    words = pl.pallas_call(
        _pack_kernel,
        grid=(nb + 2,),
        in_specs=[pl.BlockSpec((None, PACK_ROWS, d), lambda i: (layer, jnp.clip(i - 1, 0, nb - 1), 0))],
        out_specs=pl.BlockSpec((PACK_ROWS, d // 2), lambda i: (i, 0)),
        out_shape=jax.ShapeDtypeStruct((n + 2 * PACK_ROWS, d // 2), jnp.uint32),
        compiler_params=_cparams(1),
        name="pack_table",
    )(tabs)
    return words.reshape((n + 2 * PACK_ROWS) * ROW_WORDS, LANES)


def _slot_masks():
    col = lax.broadcasted_iota(jnp.int32, (8, PEER_SLOTS * 8), 1) % 8
    row = lax.broadcasted_iota(jnp.int32, (8, PEER_SLOTS * 8), 0)
    return col == 2 * (row % ROW_WORDS) + row // ROW_WORDS


def _split_bf16(x):
    hi = x.astype(BF16)
    return hi, (x - hi.astype(F32)).astype(BF16)


def _gather_rows(eid_ref, tab_ref, t):
    low_half = lax.broadcasted_iota(jnp.int32, (8, LANES), 0) < ROW_WORDS
    pairs = []
    for j in range(0, PEER_SLOTS, 2):
        even = pl.multiple_of(eid_ref[t, j], ROW_WORDS)
        odd = pl.multiple_of(eid_ref[t, j + 1], ROW_WORDS)
        pairs.append(jnp.where(low_half, tab_ref[pl.ds(even, 8), :], tab_ref[pl.ds(odd, 8), :]))
    return pltpu.bitcast(jnp.concatenate(pairs, axis=0), BF16)


def _index_copy(eid_hbm, buf, sem, first_token):
    return pltpu.make_async_copy(eid_hbm.at[pl.ds(first_token, buf.shape[0])], buf, sem)


def _for_tokens(tm, eid_hbm, idx_bufs, sems, token):
    step, nsteps = pl.program_id(0), pl.num_programs(0)
    half = tm // 2
    first = step * tm

    @pl.when(step == 0)
    def _():
        _index_copy(eid_hbm, idx_bufs[0], sems.at[0], 0).start()

    _index_copy(eid_hbm, idx_bufs[1], sems.at[1], first + half).start()
    _index_copy(eid_hbm, idx_bufs[0], sems.at[0], first).wait()
    for t in range(half):
        token(t, idx_bufs[0], t)

    @pl.when(step + 1 < nsteps)
    def _():
        _index_copy(eid_hbm, idx_bufs[0], sems.at[0], first + tm).start()

    _index_copy(eid_hbm, idx_bufs[1], sems.at[1], first + half).wait()
    for t in range(half, tm):
        token(t, idx_bufs[1], t - half)


def _index_scratch(tm):
    half = (tm // 2, PEER_SLOTS)
    return [pltpu.SMEM(half, jnp.int32), pltpu.SMEM(half, jnp.int32), pltpu.SemaphoreType.DMA((2,))]


def _peer_u_kernel(eid_hbm, tab_ref, hn_ref, gate_ref, w_ref, r_scr, idx0, idx1, sems):
    tm = hn_ref.shape[0]
    own = _slot_masks()

    def token(t, idx_ref, row):
        rows = _gather_rows(idx_ref, tab_ref, row)
        h_tile = jnp.concatenate([hn_ref[t:t + 1, p * LANES:(p + 1) * LANES] for p in range(8)], axis=0)
        h_hi, h_lo = _split_bf16(h_tile)
        d = lax.dot_general(jnp.concatenate([h_hi, h_lo], axis=0), rows, (((1,), (1,)), ((), ())),
                            preferred_element_type=F32)
        d = d[0:8] + d[8:16]
        r_scr[pl.ds(t, 1), :] = jnp.sum(jnp.where(own, d, 0.0), axis=0, keepdims=True)

    _for_tokens(tm, eid_hbm, (idx0, idx1), sems, token)
    fold = (lax.broadcasted_iota(jnp.int32, (PEER_SLOTS * 8, PEER_SLOTS), 0) // 8
            == lax.broadcasted_iota(jnp.int32, (PEER_SLOTS * 8, PEER_SLOTS), 1)).astype(BF16)
    r_hi, r_lo = _split_bf16(r_scr[...])
    a = jnp.dot(r_hi, fold, preferred_element_type=F32) + jnp.dot(r_lo, fold, preferred_element_type=F32)
    act = 0.5 * a * (1.0 + lax.erf(a * (2.0 ** -0.5)))
    w_ref[...] = gate_ref[...] * act


def _peer_u(eid, table, hn, gate, tm):
    t, d = hn.shape
    return pl.pallas_call(
        _peer_u_kernel,
        grid=(t // tm,),
        in_specs=[pl.BlockSpec(memory_space=pl.ANY),
                  pl.BlockSpec(memory_space=pltpu.VMEM),
                  pl.BlockSpec((tm, d), lambda i: (i, 0)),
                  pl.BlockSpec((tm, PEER_SLOTS), lambda i: (i, 0))],
        out_specs=pl.BlockSpec((tm, PEER_SLOTS), lambda i: (i, 0)),
        out_shape=jax.ShapeDtypeStruct((t, PEER_SLOTS), F32),
        scratch_shapes=[pltpu.VMEM((tm, PEER_SLOTS * 8), F32)] + _index_scratch(tm),
        compiler_params=_cparams(1),
        name="peer_u",
    )(eid, table, hn, gate)


def _peer_v_kernel(eid_hbm, w_ref, tab_ref, x_ref, out_ref, whi_scr, wlo_scr, idx0, idx1, sems):
    tm = x_ref.shape[0]
    own = _slot_masks()
    spread = (lax.broadcasted_iota(jnp.int32, (PEER_SLOTS, PEER_SLOTS * 8), 1) // 8
              == lax.broadcasted_iota(jnp.int32, (PEER_SLOTS, PEER_SLOTS * 8), 0)).astype(BF16)
    w_hi, w_lo = _split_bf16(w_ref[...])
    whi_scr[...] = jnp.dot(w_hi, spread, preferred_element_type=F32)
    wlo_scr[...] = jnp.dot(w_lo, spread, preferred_element_type=F32)

    def token(t, idx_ref, row):
        rows = _gather_rows(idx_ref, tab_ref, row)
        lhs = jnp.concatenate([jnp.where(own, whi_scr[pl.ds(t, 1), :], 0.0),
                               jnp.where(own, wlo_scr[pl.ds(t, 1), :], 0.0)], axis=0).astype(BF16)
        o = jnp.dot(lhs, rows, preferred_element_type=F32)
        o = o[0:8] + o[8:16]
        for p in range(8):
            cols = slice(p * LANES, (p + 1) * LANES)
            out_ref[t:t + 1, cols] = x_ref[t:t + 1, cols] + o[p:p + 1]

    _for_tokens(tm, eid_hbm, (idx0, idx1), sems, token)


def _peer_v(eid, w, table, x, tm):
    t, d = x.shape
    return pl.pallas_call(
        _peer_v_kernel,
        grid=(t // tm,),
        in_specs=[pl.BlockSpec(memory_space=pl.ANY),
                  pl.BlockSpec((tm, PEER_SLOTS), lambda i: (i, 0)),
                  pl.BlockSpec(memory_space=pltpu.VMEM),
                  pl.BlockSpec((tm, d), lambda i: (i, 0))],
        out_specs=pl.BlockSpec((tm, d), lambda i: (i, 0)),
        out_shape=jax.ShapeDtypeStruct(x.shape, F32),
        scratch_shapes=[pltpu.VMEM((tm, PEER_SLOTS * 8), F32),
                        pltpu.VMEM((tm, PEER_SLOTS * 8), F32)] + _index_scratch(tm),
        compiler_params=_cparams(1),
        name="peer_v",
    )(eid, w, table, x)


def _final_norm_kernel(x_ref, g_ref, o_ref):
    o_ref[...] = _rms(x_ref[...], g_ref[...])


def _final_norm(x, g, tm):
    t, d = x.shape
    return pl.pallas_call(
        _final_norm_kernel,
        grid=(t // tm,),
        in_specs=[pl.BlockSpec((tm, d), lambda i: (i, 0)), pl.BlockSpec((1, d), lambda i: (0, 0))],
        out_specs=pl.BlockSpec((tm, d), lambda i: (i, 0)),
        out_shape=jax.ShapeDtypeStruct((t, d), F32),
        compiler_params=_cparams(1),
        name="final_norm",
    )(x, g)


def _rope_angles(pos, dim):
    inv = ROPE_THETA ** (-jnp.arange(0, dim, 2, dtype=F32) / dim)
    ang = pos.astype(F32)[:, None] * inv[None, :]
    return jnp.cos(ang), jnp.sin(ang)


def _rope_table(parts, s_len):
    cs, sn, sp = [], [], []
    for width, cos, sin, half in parts:
        if cos is None:
            cs.append(jnp.ones((s_len, width), F32))
            sn.append(jnp.zeros((s_len, width), F32))
            sp.append(jnp.zeros((s_len, width), F32))
            continue
        reps = width // (2 * half)
        zero = jnp.zeros_like(sin)
        cs.append(jnp.tile(jnp.concatenate([cos, cos], axis=-1), (1, reps)))
        sn.append(jnp.tile(jnp.concatenate([-sin, zero], axis=-1), (1, reps)))
        sp.append(jnp.tile(jnp.concatenate([zero, sin], axis=-1), (1, reps)))
    return jnp.stack([jnp.concatenate(cs, -1), jnp.concatenate(sn, -1), jnp.concatenate(sp, -1)])


def kernel(x, g_mix, w_in, win_sink, ax_q_norm, ax_k_norm, mla_q_norm, mla_kv_norm, mla_w_uq, mla_w_ukv,
           g_group, w_out, g_ffn, peer_w_q, peer_sub_keys, peer_u, peer_v, g_final):
    b, s, d = x.shape
    t = b * s
    depth = w_in.shape[0]
    hd = HEAD_DIM

    pos = jnp.arange(s)
    cos1, sin1 = _rope_angles(pos, hd)
    cos_m, sin_m = _rope_angles(pos, MLA_ROPE)
    cos_r, sin_r = _rope_angles(pos // GRID_W, hd // 2)
    cos_c, sin_c = _rope_angles(pos % GRID_W, hd // 2)
    taba = _rope_table([(LANES, cos1, sin1, hd // 2)], s)
    tabb = _rope_table([(32, cos_r, sin_r, 16), (32, cos_c, sin_c, 16)] * 2, s)
    tabd = _rope_table([(MLA_NOPE, None, None, 0), (MLA_ROPE, cos_m, sin_m, 16), (32, None, None, 0)], s)
    head_mean = jnp.asarray(np.kron(np.eye(LANES // hd), np.full((hd, hd), 1.0 / hd)), F32)

    tm_prep = min(256, s)
    tm_tok = 256
    tm_gather = 256
    xf = x
    for l in range(depth):
        w_l = w_in[l]
        kr_cols = jnp.concatenate([jnp.zeros((d, MLA_NOPE), F32), w_l[:, 1920:1952], jnp.zeros((d, 32), F32)], -1)
        w_pad = jnp.concatenate([w_l[:, :1920], kr_cols], axis=-1).astype(BF16)
        wuq = mla_w_uq[l].reshape(-1, 4, MLA_NOPE + MLA_ROPE)
        wuq = jnp.pad(wuq, ((0, 0), (0, 0), (0, MLA_PAD - MLA_NOPE - MLA_ROPE))).reshape(-1, 4 * MLA_PAD).astype(BF16)
        wukv = mla_w_ukv[l].reshape(-1, 4, MLA_NOPE + hd)
        wuk = jnp.pad(wukv[:, :, :MLA_NOPE], ((0, 0), (0, 0), (0, MLA_PAD - MLA_NOPE))).reshape(-1, 4 * MLA_PAD).astype(BF16)
        wuv = wukv[:, :, MLA_NOPE:].reshape(-1, 4 * hd).astype(BF16)
        gq = jnp.tile(ax_q_norm[l], 2)[None]
        gk = jnp.tile(ax_k_norm[l], 2)[None]

        (qa, ka, va, qb, kb, vb, qc, kc, vc, qc4, kc4, vc4, qc16, kc16, vc16, qd, kd, vd) = _prep(
            xf.reshape(b, s, d), g_mix[l][None], w_pad, taba, tabb, tabd, gq, gk,
            mla_q_norm[l][None], mla_kv_norm[l][None], wuq, wuk, wuv, head_mean, tm_prep)

        oa = _banded(qa, ka, va, WIN_RADIUS, sink=win_sink[l])
        ob = _dense(qb, kb, vb, min(512, s))
        od = _dense(qd, kd, vd, min(512, s))
        dil = [a.reshape(t, 4 * hd) for a in _banded(qc, kc, vc, DIL_RADIUS)]
        for stride, (q_r, k_r, v_r) in zip(DIL_STRIDES[1:], ((qc4, kc4, vc4), (qc16, kc16, vc16))):
            seqs = lambda a: a.reshape((b * stride,) + a.shape[2:])
            o, lse = _banded(seqs(q_r), seqs(k_r), seqs(v_r), DIL_RADIUS)
            dil += [o.reshape(b, stride, s // stride, 4 * hd), lse.reshape(b, stride, s // stride, 4 * hd)]

        xf = _mix_out(xf.reshape(t, d), oa.reshape(t, 4 * hd), ob.reshape(t, 4 * hd), od.reshape(t, 4 * hd),
                      *dil, g_group[l], w_out[l].astype(BF16), tm_tok, b)

        hn, eid, gate = _peer_sel(xf, g_ffn[l][None], peer_w_q[l].astype(BF16),
                                  peer_sub_keys[l].astype(BF16), tm_tok)
        wgt = _peer_u(eid, _pack_table(peer_u, l), hn, gate, tm_gather)
        xf = _peer_v(eid, wgt, _pack_table(peer_v, l), xf, tm_gather)

    return _final_norm(xf, g_final[None], tm_tok).reshape(b, s, d)
```
